```python
import jax, jax.numpy as jnp
from jax import lax
import numpy as np

D_MODEL = 2048
BATCH = 8
SEQ = 4096
DEPTH = 2

N_META = 16
D_LRU = D_MODEL // 2
LRU_HEADS = 8
LRU_HEAD_DIM = D_LRU // LRU_HEADS
CONV_WIDTH = 4
LRU_C = 8.0
D_POOL = D_MODEL // 2
POOL_WINDOWS = (2, 4, 8, 16)
POOL_GROUPS = len(POOL_WINDOWS)
POOL_GROUP_DIM = D_POOL // POOL_GROUPS
D_MIX = D_LRU + D_POOL
D_IN = 2 * D_LRU + D_POOL
D_FF = ((8 * D_MODEL // 3 + 255) // 256) * 256
RMS_EPS = 1e-6

kernel_name = "hymba_rglru_pool_macaron"


def rms_norm(x, g):
    xf = x.astype(jnp.float32)
    y = xf * lax.rsqrt(jnp.mean(xf * xf, axis=-1, keepdims=True) + RMS_EPS)
    return (y * g.astype(jnp.float32)).astype(x.dtype)


def swiglu(h, w_in, w_out):
    g, u = jnp.split(h @ w_in, 2, axis=-1)
    return (jax.nn.silu(g) * u) @ w_out


def causal_dwconv(x, w, b):
    T = x.shape[1]
    xp = jnp.pad(x, ((0, 0), (CONV_WIDTH - 1, 0), (0, 0)))
    y = b
    for k in range(CONV_WIDTH):
        y = y + xp[:, k:k + T] * w[k]
    return y


def rg_lru(x, wa, ba, wx, bx, a_param):
    B, T, _ = x.shape
    xh = x.reshape(B, T, LRU_HEADS, LRU_HEAD_DIM)
    r = jax.nn.sigmoid((jnp.einsum('bthi,hij->bthj', xh, wa).reshape(B, T, D_LRU) + ba).astype(jnp.float32))
    i = jax.nn.sigmoid((jnp.einsum('bthi,hij->bthj', xh, wx).reshape(B, T, D_LRU) + bx).astype(jnp.float32))
    log_a = -LRU_C * r * jax.nn.softplus(-a_param.astype(jnp.float32))
    a = jnp.exp(log_a)
    mult = jnp.sqrt(-jnp.expm1(2.0 * log_a))
    bt = mult * i * x.astype(jnp.float32)

    def combine(left, right):
        a1, b1 = left
        a2, b2 = right
        return a1 * a2, a2 * b1 + b2

    _, h = lax.associative_scan(combine, (a, bt), axis=1)
    return h.astype(x.dtype)


def multiscale_pool(u, w, b, scale):
    B, T, _ = u.shape
    ug = u.reshape(B, T, POOL_GROUPS, POOL_GROUP_DIM).astype(jnp.float32)
    cs = jnp.cumsum(ug, axis=1)
    t1 = jnp.arange(1, T + 1, dtype=jnp.float32)
    outs = []
    for g, win in enumerate(POOL_WINDOWS):
        c = cs[:, :, g]
        lag = jnp.pad(c[:, :T - win], ((0, 0), (win, 0), (0, 0)))
        cnt = jnp.minimum(t1, float(win))[None, :, None]
        outs.append((c - lag) / cnt - ug[:, :, g])
    d = jnp.stack(outs, axis=2).astype(u.dtype)
    y = jnp.einsum('btgi,gij->btgj', d, w).reshape(B, T, D_POOL) + b
    return y * scale


def setup_inputs(seed: int = 0) -> dict:
    key = jax.random.key(seed)
    ks = iter(jax.random.split(key, 32))
    f32 = jnp.float32

    def nrm(shape, fan_in):
        return jax.random.normal(next(ks), shape, f32) * (fan_in ** -0.5)

    def gain(shape):
        return 1.0 + 0.02 * jax.random.normal(next(ks), shape, f32)

    def bias(shape):
        return 0.01 * jax.random.normal(next(ks), shape, f32)

    L = DEPTH
    x = jax.random.normal(next(ks), (BATCH, SEQ, D_MODEL), f32)
    meta_tokens = jax.random.normal(next(ks), (N_META, D_MODEL), f32)
    ffn1_norm = gain((L, D_MODEL))
    ffn1_w_in = nrm((L, D_MODEL, 2 * D_FF), D_MODEL)
    ffn1_w_out = nrm((L, D_FF, D_MODEL), D_FF)
    mix_norm = gain((L, D_MODEL))
    w_in = nrm((L, D_MODEL, D_IN), D_MODEL)
    conv_w = nrm((L, CONV_WIDTH, D_LRU), CONV_WIDTH)
    conv_b = bias((L, D_LRU))
    lru_wa = nrm((L, LRU_HEADS, LRU_HEAD_DIM, LRU_HEAD_DIM), LRU_HEAD_DIM)
    lru_ba = bias((L, D_LRU))
    lru_wx = nrm((L, LRU_HEADS, LRU_HEAD_DIM, LRU_HEAD_DIM), LRU_HEAD_DIM)
    lru_bx = bias((L, D_LRU))
    a8 = jax.random.uniform(next(ks), (L, D_LRU), f32, 0.9, 0.999)
    a0 = a8 ** (1.0 / LRU_C)
    lru_a_param = jnp.log(a0) - jnp.log1p(-a0)
    pool_w = nrm((L, POOL_GROUPS, POOL_GROUP_DIM, POOL_GROUP_DIM), POOL_GROUP_DIM)
    pool_b = bias((L, D_POOL))
    pool_scale = 1.0 + 0.1 * jax.random.normal(next(ks), (L, D_POOL), f32)
    w_out = nrm((L, D_MIX, D_MODEL), D_MIX)
    ffn2_norm = gain((L, D_MODEL))
    ffn2_w_in = nrm((L, D_MODEL, 2 * D_FF), D_MODEL)
    ffn2_w_out = nrm((L, D_FF, D_MODEL), D_FF)
    final_norm = gain((D_MODEL,))
    return {"x": x, "meta_tokens": meta_tokens,
            "ffn1_norm": ffn1_norm, "ffn1_w_in": ffn1_w_in, "ffn1_w_out": ffn1_w_out,
            "mix_norm": mix_norm, "w_in": w_in, "conv_w": conv_w, "conv_b": conv_b,
            "lru_wa": lru_wa, "lru_ba": lru_ba, "lru_wx": lru_wx, "lru_bx": lru_bx,
            "lru_a_param": lru_a_param, "pool_w": pool_w, "pool_b": pool_b,
            "pool_scale": pool_scale, "w_out": w_out,
            "ffn2_norm": ffn2_norm, "ffn2_w_in": ffn2_w_in, "ffn2_w_out": ffn2_w_out,
            "final_norm": final_norm}


def reference(x, meta_tokens, ffn1_norm, ffn1_w_in, ffn1_w_out, mix_norm, w_in, conv_w, conv_b,
              lru_wa, lru_ba, lru_wx, lru_bx, lru_a_param, pool_w, pool_b, pool_scale, w_out,
              ffn2_norm, ffn2_w_in, ffn2_w_out, final_norm):
    B = x.shape[0]
    meta = jnp.broadcast_to(meta_tokens.astype(x.dtype)[None], (B, N_META, D_MODEL))
    h = jnp.concatenate([meta, x], axis=1)
    for l in range(DEPTH):
        h = h + 0.5 * swiglu(rms_norm(h, ffn1_norm[l]), ffn1_w_in[l], ffn1_w_out[l])
        z = rms_norm(h, mix_norm[l]) @ w_in[l]
        zx, zg, zp = jnp.split(z, [D_LRU, 2 * D_LRU], axis=-1)
        ya = rg_lru(causal_dwconv(zx, conv_w[l], conv_b[l]),
                    lru_wa[l], lru_ba[l], lru_wx[l], lru_bx[l], lru_a_param[l]) * jax.nn.gelu(zg)
        yb = multiscale_pool(zp, pool_w[l], pool_b[l], pool_scale[l])
        h = h + jnp.concatenate([ya, yb], axis=-1) @ w_out[l]
        h = h + 0.5 * swiglu(rms_norm(h, ffn2_norm[l]), ffn2_w_in[l], ffn2_w_out[l])
    return rms_norm(h, final_norm)[:, N_META:]
```

```python
import functools

import jax
import jax.numpy as jnp
from jax import lax
from jax.experimental import pallas as pl
from jax.experimental.pallas import tpu as pltpu

D_MODEL = 2048
BATCH = 8
DEPTH = 2
N_META = 16
D_LRU = D_MODEL // 2
LRU_HEADS = 8
LRU_HEAD_DIM = D_LRU // LRU_HEADS
CONV_WIDTH = 4
LRU_C = 8.0
D_POOL = D_MODEL // 2
POOL_WINDOWS = (2, 4, 8, 16)
POOL_GROUP_DIM = D_POOL // len(POOL_WINDOWS)
D_MIX = D_LRU + D_POOL
D_IN = 2 * D_LRU + D_POOL
D_FF = ((8 * D_MODEL // 3 + 255) // 256) * 256
RMS_EPS = 1e-6

SUBLANES = 8
CONV_TAIL = (CONV_WIDTH - 1) * BATCH
POOL_TAIL = max(POOL_WINDOWS) * BATCH
V7X_VMEM_LIMIT_BYTES = 60 * 1024 * 1024

FFN_ROWS = 1024
FFN_COLS = 512
MIX_ROWS = 256

F32 = jnp.float32
BF16 = jnp.bfloat16

assert BATCH == SUBLANES
assert D_FF % FFN_COLS == 0


def _rms_norm(x, gain):
    return x * lax.rsqrt(jnp.mean(x * x, axis=-1, keepdims=True) + RMS_EPS) * gain


def _sigmoid(x):
    return 0.5 * jnp.tanh(0.5 * x) + 0.5


def _gelu_tanh(x):
    return 0.5 * x * (1.0 + jnp.tanh(0.7978845608028654 * (x + 0.044715 * (x * x * x))))


def _ffn_kernel(x_ref, gain_ref, wg_ref, wu_ref, wo_ref, *rest, final_norm):
    if final_norm:
        fgain_ref, o_ref, n_ref = rest
    else:
        o_ref, n_ref = rest
    j = pl.program_id(1)

    @pl.when(j == 0)
    def _():
        x = x_ref[...]
        n_ref[...] = _rms_norm(x, gain_ref[...]).astype(BF16)
        o_ref[...] = x

    n = n_ref[...]
    g = jnp.dot(n, wg_ref[...], preferred_element_type=F32)
    u = jnp.dot(n, wu_ref[...], preferred_element_type=F32)
    act = (g * (0.25 * jnp.tanh(0.5 * g) + 0.25)) * u
    o_ref[...] += jnp.dot(act.astype(BF16), wo_ref[...], preferred_element_type=F32)

    if final_norm:
        @pl.when(j == pl.num_programs(1) - 1)
        def _():
            o_ref[...] = _rms_norm(o_ref[...], fgain_ref[...])


def _ffn(h, gain, w_in, w_out, *, rows, final_gain=None):
    m = h.shape[0]
    nf = D_FF // FFN_COLS
    final_norm = final_gain is not None
    in_specs = [
        pl.BlockSpec((rows, D_MODEL), lambda i, j: (i, 0)),
        pl.BlockSpec((1, D_MODEL), lambda i, j: (0, 0)),
        pl.BlockSpec((D_MODEL, FFN_COLS), lambda i, j: (0, j)),
        pl.BlockSpec((D_MODEL, FFN_COLS), lambda i, j: (0, nf + j)),
        pl.BlockSpec((FFN_COLS, D_MODEL), lambda i, j: (j, 0)),
    ]
    args = [h, gain, w_in, w_in, w_out]
    if final_norm:
        in_specs.append(pl.BlockSpec((1, D_MODEL), lambda i, j: (0, 0)))
        args.append(final_gain)
    return pl.pallas_call(
        functools.partial(_ffn_kernel, final_norm=final_norm),
        grid=(m // rows, nf),
        in_specs=in_specs,
        out_specs=pl.BlockSpec((rows, D_MODEL), lambda i, j: (i, 0)),
        out_shape=jax.ShapeDtypeStruct((m, D_MODEL), F32),
        scratch_shapes=[pltpu.VMEM((rows, D_MODEL), BF16)],
        compiler_params=pltpu.CompilerParams(
            dimension_semantics=("parallel", "arbitrary"),
            vmem_limit_bytes=V7X_VMEM_LIMIT_BYTES),
        name="ffn_final" if final_norm else "ffn",
    )(*args)


def _mix_kernel(x_ref, gain_ref, win_ref, convw_ref, convb_ref, wgate_ref, ba_ref, bx_ref,
                aparam_ref, poolw_ref, poolb_ref, pools_ref, wout_ref,
                h0_ref, ctail0_ref, ptail0_ref,
                o_ref, hstate_ref, ctail_ref, ptail_ref,
                zx_ext, zp_ext, a_s, b_s, mix_s, *, rows, t_offset):
    i = pl.program_id(0)
    steps = rows // BATCH

    @pl.when(i == 0)
    def _():
        hstate_ref[...] = h0_ref[...]
        zx_ext[0:CONV_TAIL, :] = ctail0_ref[...]
        zp_ext[0:POOL_TAIL, :] = ptail0_ref[...]

    x = x_ref[...]
    n = _rms_norm(x, gain_ref[...]).astype(BF16)
    z = jnp.dot(n, win_ref[...], preferred_element_type=F32)
    zx = z[:, :D_LRU]
    zg = z[:, D_LRU:2 * D_LRU]
    zp = z[:, 2 * D_LRU:]

    zx_ext[CONV_TAIL:CONV_TAIL + rows, :] = zx
    xc = convb_ref[...] + convw_ref[CONV_WIDTH - 1:CONV_WIDTH, :] * zx
    for k in range(CONV_WIDTH - 1):
        xc = xc + convw_ref[k:k + 1, :] * zx_ext[k * BATCH:k * BATCH + rows, :]
    ctail = zx_ext[rows:rows + CONV_TAIL, :]
    zx_ext[0:CONV_TAIL, :] = ctail
    ctail_ref[...] = ctail

    ap = aparam_ref[...]
    neg_c_softplus = -LRU_C * (jnp.maximum(-ap, 0.0) + jnp.log1p(jnp.exp(-jnp.abs(ap))))
    xcb = xc.astype(BF16)
    for hd in range(LRU_HEADS):
        cols = slice(hd * LRU_HEAD_DIM, (hd + 1) * LRU_HEAD_DIM)
        gates = jnp.dot(xcb[:, cols], wgate_ref[hd], preferred_element_type=F32)
        r = _sigmoid(gates[:, :LRU_HEAD_DIM] + ba_ref[:, cols])
        ig = _sigmoid(gates[:, LRU_HEAD_DIM:] + bx_ref[:, cols])
        t = jnp.tanh(0.5 * (r * neg_c_softplus[:, cols]))
        q = 1.0 / (1.0 - t)
        a_s[:, cols] = (1.0 + t) * q
        b_s[:, cols] = (2.0 * q * jnp.sqrt(-t)) * ig * xc[:, cols]

    def scan_step(s, h):
        rs = pl.ds(pl.multiple_of(s * BATCH, BATCH), BATCH)
        h = a_s[rs, :] * h + b_s[rs, :]
        b_s[rs, :] = h
        return h

    hstate_ref[...] = lax.fori_loop(0, steps, scan_step, hstate_ref[...], unroll=8)
    mix_s[:, :D_LRU] = (b_s[...] * _gelu_tanh(zg)).astype(BF16)

    zp_ext[POOL_TAIL:POOL_TAIL + rows, :] = zp
    for gidx, win in enumerate(POOL_WINDOWS):
        cols = slice(gidx * POOL_GROUP_DIM, (gidx + 1) * POOL_GROUP_DIM)
        ext = zp_ext[:, cols]
        s = ext
        span = 1
        while span < win:
            s = s[span * BATCH:, :] + s[:s.shape[0] - span * BATCH, :]
            span *= 2
        s = s[s.shape[0] - rows:, :]
        u = ext[POOL_TAIL:, :]
        if t_offset + 1 >= win:
            d = s * (1.0 / win) - u
        else:
            row = lax.broadcasted_iota(jnp.int32, (rows, POOL_GROUP_DIM), 0)
            t_abs = lax.shift_right_logical(row, 3) + (i * steps + t_offset)
            d = s / jnp.minimum(t_abs + 1, win).astype(F32) - u
        y = jnp.dot(d.astype(BF16), poolw_ref[gidx], preferred_element_type=F32)
        y = (y + poolb_ref[:, cols]) * pools_ref[:, cols]
        mix_s[:, D_LRU + gidx * POOL_GROUP_DIM:D_LRU + (gidx + 1) * POOL_GROUP_DIM] = y.astype(BF16)
    ptail = zp_ext[rows:rows + POOL_TAIL, :]
    zp_ext[0:POOL_TAIL, :] = ptail
    ptail_ref[...] = ptail

    o_ref[...] = x + jnp.dot(mix_s[...], wout_ref[...], preferred_element_type=F32)


def _mix(h, p, state, *, rows, t_offset):
    m = h.shape[0]
    const = lambda i: (0, 0)
    const3 = lambda i: (0, 0, 0)

    def resident(shape):
        return pl.BlockSpec(shape, const if len(shape) == 2 else const3,
                            pipeline_mode=pl.Buffered(1))

    vec = lambda nl: pl.BlockSpec((1, nl), const)
    in_specs = [
        pl.BlockSpec((rows, D_MODEL), lambda i: (i, 0)),
        vec(D_MODEL),
        resident((D_MODEL, D_IN)),
        pl.BlockSpec((CONV_WIDTH, D_LRU), const),
        vec(D_LRU),
        resident((LRU_HEADS, LRU_HEAD_DIM, 2 * LRU_HEAD_DIM)),
        vec(D_LRU), vec(D_LRU), vec(D_LRU),
        resident((len(POOL_WINDOWS), POOL_GROUP_DIM, POOL_GROUP_DIM)),
        vec(D_POOL), vec(D_POOL),
        resident((D_MIX, D_MODEL)),
        pl.BlockSpec((BATCH, D_LRU), const),
        pl.BlockSpec((CONV_TAIL, D_LRU), const),
        pl.BlockSpec((POOL_TAIL, D_POOL), const),
    ]
    out_specs = [
        pl.BlockSpec((rows, D_MODEL), lambda i: (i, 0)),
        pl.BlockSpec((BATCH, D_LRU), const),
        pl.BlockSpec((CONV_TAIL, D_LRU), const),
        pl.BlockSpec((POOL_TAIL, D_POOL), const),
    ]
    out_shape = [
        jax.ShapeDtypeStruct((m, D_MODEL), F32),
        jax.ShapeDtypeStruct((BATCH, D_LRU), F32),
        jax.ShapeDtypeStruct((CONV_TAIL, D_LRU), F32),
        jax.ShapeDtypeStruct((POOL_TAIL, D_POOL), F32),
    ]
    scratch = [
        pltpu.VMEM((rows + CONV_TAIL, D_LRU), F32),
        pltpu.VMEM((rows + POOL_TAIL, D_POOL), F32),
        pltpu.VMEM((rows, D_LRU), F32),
        pltpu.VMEM((rows, D_LRU), F32),
        pltpu.VMEM((rows, D_MIX), BF16),
    ]
    out, hstate, ctail, ptail = pl.pallas_call(
        functools.partial(_mix_kernel, rows=rows, t_offset=t_offset),
        grid=(m // rows,),
        in_specs=in_specs,
        out_specs=out_specs,
        out_shape=out_shape,
        scratch_shapes=scratch,
        compiler_params=pltpu.CompilerParams(
            dimension_semantics=("arbitrary",),
            vmem_limit_bytes=V7X_VMEM_LIMIT_BYTES),
        name="mix",
    )(h, p["gain"], p["w_in"], p["conv_w"], p["conv_b"], p["w_gate"], p["ba"], p["bx"],
      p["a_param"], p["pool_w"], p["pool_b"], p["pool_scale"], p["w_out"], *state)
    return out, (hstate, ctail, ptail)


def kernel(x, meta_tokens, ffn1_norm, ffn1_w_in, ffn1_w_out, mix_norm, w_in, conv_w, conv_b,
           lru_wa, lru_ba, lru_wx, lru_bx, lru_a_param, pool_w, pool_b, pool_scale, w_out,
           ffn2_norm, ffn2_w_in, ffn2_w_out, final_norm):
    b, t, d = x.shape
    assert (b, d) == (BATCH, D_MODEL) and (t * b) % FFN_ROWS == 0 and (t * b) % MIX_ROWS == 0

    hx = jnp.transpose(x, (1, 0, 2)).reshape(t * b, d)
    hm = jnp.broadcast_to(meta_tokens.astype(x.dtype)[:, None, :], (N_META, b, d)).reshape(N_META * b, d)
    meta_rows = N_META * b

    row = lambda v: v.reshape(1, -1)
    zero_state = (jnp.zeros((BATCH, D_LRU), F32), jnp.zeros((CONV_TAIL, D_LRU), F32),
                  jnp.zeros((POOL_TAIL, D_POOL), F32))

    for l in range(DEPTH):
        last = l == DEPTH - 1
        f1 = (row(ffn1_norm[l]), ffn1_w_in[l].astype(BF16), ffn1_w_out[l].astype(BF16))
        f2 = (row(ffn2_norm[l]), ffn2_w_in[l].astype(BF16), ffn2_w_out[l].astype(BF16))
        mp = dict(
            gain=row(mix_norm[l]), w_in=w_in[l].astype(BF16), conv_w=conv_w[l], conv_b=row(conv_b[l]),
            w_gate=jnp.concatenate([lru_wa[l], lru_wx[l]], axis=-1).astype(BF16),
            ba=row(lru_ba[l]), bx=row(lru_bx[l]), a_param=row(lru_a_param[l]),
            pool_w=pool_w[l].astype(BF16), pool_b=row(pool_b[l]), pool_scale=row(pool_scale[l]),
            w_out=w_out[l].astype(BF16))

        hm = _ffn(hm, *f1, rows=meta_rows)
        hx = _ffn(hx, *f1, rows=FFN_ROWS)
        hm, meta_state = _mix(hm, mp, zero_state, rows=meta_rows, t_offset=0)
        hx, _ = _mix(hx, mp, meta_state, rows=MIX_ROWS, t_offset=N_META)
        if not last:
            hm = _ffn(hm, *f2, rows=meta_rows)
        hx = _ffn(hx, *f2, rows=FFN_ROWS, final_gain=row(final_norm) if last else None)

    return jnp.transpose(hx.reshape(t, b, d), (1, 0, 2))
```

```python
import functools

import jax
import jax.numpy as jnp
from jax import lax
from jax.experimental import pallas as pl
from jax.experimental.pallas import tpu as pltpu

D_MODEL = 2048
BATCH = 8
DEPTH = 2
N_META = 16
D_LRU = D_MODEL // 2
LRU_HEADS = 8
LRU_HEAD_DIM = D_LRU // LRU_HEADS
CONV_WIDTH = 4
LRU_C = 8.0
D_POOL = D_MODEL // 2
POOL_WINDOWS = (2, 4, 8, 16)
POOL_GROUP_DIM = D_POOL // len(POOL_WINDOWS)
D_MIX = D_LRU + D_POOL
D_IN = 2 * D_LRU + D_POOL
D_FF = ((8 * D_MODEL // 3 + 255) // 256) * 256
RMS_EPS = 1e-6

SUBLANES = 8
CONV_TAIL = (CONV_WIDTH - 1) * BATCH
POOL_TAIL = max(POOL_WINDOWS) * BATCH
V7X_VMEM_LIMIT_BYTES = 60 * 1024 * 1024

FFN_ROWS = 1024
FFN_COLS = 512
MIX_ROWS = 256
NORM_STEPS = 16

F32 = jnp.float32
BF16 = jnp.bfloat16

assert BATCH == SUBLANES
assert D_FF % FFN_COLS == 0


def _rms_norm(x, gain):
    return x * lax.rsqrt(jnp.mean(x * x, axis=-1, keepdims=True) + RMS_EPS) * gain


def _sigmoid(x):
    return 0.5 * jnp.tanh(0.5 * x) + 0.5


def _gelu_tanh(x):
    return 0.5 * x * (1.0 + jnp.tanh(0.7978845608028654 * (x + 0.044715 * (x * x * x))))


def _ffn_kernel(x_ref, gain_ref, wg_ref, wu_ref, wo_ref, *rest, rows, batch_major_in,
                batch_major_out, final_norm):
    if final_norm:
        fgain_ref, o_ref, n_ref = rest
    else:
        o_ref, n_ref = rest
    j = pl.program_id(1)
    steps = rows // BATCH

    @pl.when(j == 0)
    def _():
        def piece(s, carry):
            t0 = pl.multiple_of(s * NORM_STEPS, NORM_STEPS)
            r0 = pl.multiple_of(s * NORM_STEPS * BATCH, NORM_STEPS * BATCH)
            if batch_major_in:
                x = jnp.swapaxes(x_ref[:, pl.ds(t0, NORM_STEPS), :], 0, 1)
                x = x.reshape(NORM_STEPS * BATCH, D_MODEL)
            else:
                x = x_ref[pl.ds(r0, NORM_STEPS * BATCH), :]
            if batch_major_out:
                x = jnp.swapaxes(x.reshape(NORM_STEPS, BATCH, D_MODEL), 0, 1)
                o_ref[:, pl.ds(t0, NORM_STEPS), :] = x
                n_ref[:, pl.ds(t0, NORM_STEPS), :] = _rms_norm(x, gain_ref[...]).astype(BF16)
            else:
                o_ref[pl.ds(r0, NORM_STEPS * BATCH), :] = x
                n_ref[pl.ds(r0, NORM_STEPS * BATCH), :] = _rms_norm(x, gain_ref[...]).astype(BF16)
            return carry

        lax.fori_loop(0, steps // NORM_STEPS, piece, 0)

    n = n_ref[...].reshape(rows, D_MODEL)
    g = jnp.dot(n, wg_ref[...], preferred_element_type=F32)
    u = jnp.dot(n, wu_ref[...], preferred_element_type=F32)
    act = (g * (0.25 * jnp.tanh(0.5 * g) + 0.25)) * u
    o_ref[...] += jnp.dot(act.astype(BF16), wo_ref[...],
                          preferred_element_type=F32).reshape(o_ref.shape)

    if final_norm:
        @pl.when(j == pl.num_programs(1) - 1)
        def _():
            o_ref[...] = _rms_norm(o_ref[...], fgain_ref[...])


def _ffn(h, gain, w_in, w_out, layer, *, rows, batch_major_in=False, batch_major_out=False,
         final_gain=None):
    m = h.shape[1] * BATCH if batch_major_in else h.shape[0]
    steps = rows // BATCH
    nf = D_FF // FFN_COLS
    final_norm = final_gain is not None
    row_spec = pl.BlockSpec((rows, D_MODEL), lambda i, j: (i, 0))
    bt_spec = pl.BlockSpec((BATCH, steps, D_MODEL), lambda i, j: (0, i, 0))
    in_specs = [
        bt_spec if batch_major_in else row_spec,
        pl.BlockSpec((1, D_MODEL), lambda i, j: (0, 0)),
        pl.BlockSpec((None, D_MODEL, FFN_COLS), lambda i, j: (layer, 0, j)),
        pl.BlockSpec((None, D_MODEL, FFN_COLS), lambda i, j: (layer, 0, nf + j)),
        pl.BlockSpec((None, FFN_COLS, D_MODEL), lambda i, j: (layer, j, 0)),
    ]
    args = [h, gain, w_in, w_in, w_out]
    if final_norm:
        in_specs.append(pl.BlockSpec((1, D_MODEL), lambda i, j: (0, 0)))
        args.append(final_gain)
    out_shape = (BATCH, m // BATCH, D_MODEL) if batch_major_out else (m, D_MODEL)
    return pl.pallas_call(
        functools.partial(_ffn_kernel, rows=rows, batch_major_in=batch_major_in,
                          batch_major_out=batch_major_out, final_norm=final_norm),
        grid=(m // rows, nf),
        in_specs=in_specs,
        out_specs=bt_spec if batch_major_out else row_spec,
        out_shape=jax.ShapeDtypeStruct(out_shape, F32),
        scratch_shapes=[pltpu.VMEM((BATCH, steps, D_MODEL) if batch_major_out else (rows, D_MODEL),
                                   BF16)],
        compiler_params=pltpu.CompilerParams(
            dimension_semantics=("parallel", "arbitrary"),
            vmem_limit_bytes=V7X_VMEM_LIMIT_BYTES),
        name="ffn" + ("_from_bt" if batch_major_in else "") + ("_to_bt" if batch_major_out else ""),
    )(*args)


def _mix_kernel(x_ref, gain_ref, win_ref, convw_ref, convb_ref, wgate_ref, ba_ref, bx_ref,
                aparam_ref, poolw_ref, poolb_ref, pools_ref, wout_ref,
                h0_ref, ctail0_ref, ptail0_ref,
                o_ref, hstate_ref, ctail_ref, ptail_ref,
                zx_ext, zp_ext, a_s, b_s, mix_s, *, rows, t_offset):
    i = pl.program_id(0)
    steps = rows // BATCH

    @pl.when(i == 0)
    def _():
        hstate_ref[...] = h0_ref[...]
        zx_ext[0:CONV_TAIL, :] = ctail0_ref[...]
        zp_ext[0:POOL_TAIL, :] = ptail0_ref[...]

    x = x_ref[...]
    n = _rms_norm(x, gain_ref[...]).astype(BF16)
    z = jnp.dot(n, win_ref[...], preferred_element_type=F32)
    zx = z[:, :D_LRU]
    zg = z[:, D_LRU:2 * D_LRU]
    zp = z[:, 2 * D_LRU:]

    zx_ext[CONV_TAIL:CONV_TAIL + rows, :] = zx
    xc = convb_ref[...] + convw_ref[CONV_WIDTH - 1:CONV_WIDTH, :] * zx
    for k in range(CONV_WIDTH - 1):
        xc = xc + convw_ref[k:k + 1, :] * zx_ext[k * BATCH:k * BATCH + rows, :]
    ctail = zx_ext[rows:rows + CONV_TAIL, :]
    zx_ext[0:CONV_TAIL, :] = ctail
    ctail_ref[...] = ctail

    ap = aparam_ref[...]
    neg_c_softplus = -LRU_C * (jnp.maximum(-ap, 0.0) + jnp.log1p(jnp.exp(-jnp.abs(ap))))
    xcb = xc.astype(BF16)
    for hd in range(LRU_HEADS):
        cols = slice(hd * LRU_HEAD_DIM, (hd + 1) * LRU_HEAD_DIM)
        gates = jnp.dot(xcb[:, cols], wgate_ref[hd], preferred_element_type=F32)
        r = _sigmoid(gates[:, :LRU_HEAD_DIM] + ba_ref[:, cols])
        ig = _sigmoid(gates[:, LRU_HEAD_DIM:] + bx_ref[:, cols])
        t = jnp.tanh(0.5 * (r * neg_c_softplus[:, cols]))
        q = 1.0 / (1.0 - t)
        a_s[:, cols] = (1.0 + t) * q
        b_s[:, cols] = (2.0 * q * jnp.sqrt(-t)) * ig * xc[:, cols]

    def scan_step(s, h):
        rs = pl.ds(pl.multiple_of(s * BATCH, BATCH), BATCH)
        h = a_s[rs, :] * h + b_s[rs, :]
        b_s[rs, :] = h
        return h

    hstate_ref[...] = lax.fori_loop(0, steps, scan_step, hstate_ref[...], unroll=8)
    mix_s[:, :D_LRU] = (b_s[...] * _gelu_tanh(zg)).astype(BF16)

    zp_ext[POOL_TAIL:POOL_TAIL + rows, :] = zp
    for gidx, win in enumerate(POOL_WINDOWS):
        cols = slice(gidx * POOL_GROUP_DIM, (gidx + 1) * POOL_GROUP_DIM)
        ext = zp_ext[:, cols]
        s = ext
        span = 1
        while span < win:
            s = s[span * BATCH:, :] + s[:s.shape[0] - span * BATCH, :]
            span *= 2
        s = s[s.shape[0] - rows:, :]
        u = ext[POOL_TAIL:, :]
        if t_offset + 1 >= win:
            d = s * (1.0 / win) - u
        else:
            row = lax.broadcasted_iota(jnp.int32, (rows, POOL_GROUP_DIM), 0)
            t_abs = lax.shift_right_logical(row, 3) + (i * steps + t_offset)
            d = s / jnp.minimum(t_abs + 1, win).astype(F32) - u
        y = jnp.dot(d.astype(BF16), poolw_ref[gidx], preferred_element_type=F32)
        y = (y + poolb_ref[:, cols]) * pools_ref[:, cols]
        mix_s[:, D_LRU + gidx * POOL_GROUP_DIM:D_LRU + (gidx + 1) * POOL_GROUP_DIM] = y.astype(BF16)
    ptail = zp_ext[rows:rows + POOL_TAIL, :]
    zp_ext[0:POOL_TAIL, :] = ptail
    ptail_ref[...] = ptail

    o_ref[...] = x + jnp.dot(mix_s[...], wout_ref[...], preferred_element_type=F32)


def _mix(h, p, layer, state, *, rows, t_offset):
    m = h.shape[0]
    const = lambda i: (0, 0)

    def resident(shape):
        return pl.BlockSpec((None,) + shape, lambda i: (layer,) + (0,) * len(shape),
                            pipeline_mode=pl.Buffered(1))

    vec = lambda nl: pl.BlockSpec((1, nl), const)
    in_specs = [
        pl.BlockSpec((rows, D_MODEL), lambda i: (i, 0)),
        vec(D_MODEL),
        resident((D_MODEL, D_IN)),
        pl.BlockSpec((CONV_WIDTH, D_LRU), const),
        vec(D_LRU),
        resident((LRU_HEADS, LRU_HEAD_DIM, 2 * LRU_HEAD_DIM)),
        vec(D_LRU), vec(D_LRU), vec(D_LRU),
        resident((len(POOL_WINDOWS), POOL_GROUP_DIM, POOL_GROUP_DIM)),
        vec(D_POOL), vec(D_POOL),
        resident((D_MIX, D_MODEL)),
        pl.BlockSpec((BATCH, D_LRU), const),
        pl.BlockSpec((CONV_TAIL, D_LRU), const),
        pl.BlockSpec((POOL_TAIL, D_POOL), const),
    ]
    out_specs = [
        pl.BlockSpec((rows, D_MODEL), lambda i: (i, 0)),
        pl.BlockSpec((BATCH, D_LRU), const),
        pl.BlockSpec((CONV_TAIL, D_LRU), const),
        pl.BlockSpec((POOL_TAIL, D_POOL), const),
    ]
    out_shape = [
        jax.ShapeDtypeStruct((m, D_MODEL), F32),
        jax.ShapeDtypeStruct((BATCH, D_LRU), F32),
        jax.ShapeDtypeStruct((CONV_TAIL, D_LRU), F32),
        jax.ShapeDtypeStruct((POOL_TAIL, D_POOL), F32),
    ]
    scratch = [
        pltpu.VMEM((rows + CONV_TAIL, D_LRU), F32),
        pltpu.VMEM((rows + POOL_TAIL, D_POOL), F32),
        pltpu.VMEM((rows, D_LRU), F32),
        pltpu.VMEM((rows, D_LRU), F32),
        pltpu.VMEM((rows, D_MIX), BF16),
    ]
    out, hstate, ctail, ptail = pl.pallas_call(
        functools.partial(_mix_kernel, rows=rows, t_offset=t_offset),
        grid=(m // rows,),
        in_specs=in_specs,
        out_specs=out_specs,
        out_shape=out_shape,
        scratch_shapes=scratch,
        compiler_params=pltpu.CompilerParams(
            dimension_semantics=("arbitrary",),
            vmem_limit_bytes=V7X_VMEM_LIMIT_BYTES),
        name="mix",
    )(h, p["gain"], p["w_in"], p["conv_w"], p["conv_b"], p["w_gate"], p["ba"], p["bx"],
      p["a_param"], p["pool_w"], p["pool_b"], p["pool_scale"], p["w_out"], *state)
    return out, (hstate, ctail, ptail)


def kernel(x, meta_tokens, ffn1_norm, ffn1_w_in, ffn1_w_out, mix_norm, w_in, conv_w, conv_b,
           lru_wa, lru_ba, lru_wx, lru_bx, lru_a_param, pool_w, pool_b, pool_scale, w_out,
           ffn2_norm, ffn2_w_in, ffn2_w_out, final_norm):
    b, t, d = x.shape
    assert (b, d) == (BATCH, D_MODEL) and (t * b) % FFN_ROWS == 0 and (t * b) % MIX_ROWS == 0

    hm = jnp.broadcast_to(meta_tokens.astype(x.dtype)[:, None, :], (N_META, b, d)).reshape(N_META * b, d)
    meta_rows = N_META * b
    hx = x

    row = lambda v: v.reshape(1, -1)
    zero_state = (jnp.zeros((BATCH, D_LRU), F32), jnp.zeros((CONV_TAIL, D_LRU), F32),
                  jnp.zeros((POOL_TAIL, D_POOL), F32))
    f1_w = (ffn1_w_in.astype(BF16), ffn1_w_out.astype(BF16))
    f2_w = (ffn2_w_in.astype(BF16), ffn2_w_out.astype(BF16))
    mix_w = dict(w_in=w_in.astype(BF16),
                 w_gate=jnp.concatenate([lru_wa, lru_wx], axis=-1).astype(BF16),
                 pool_w=pool_w.astype(BF16), w_out=w_out.astype(BF16))

    for l in range(DEPTH):
        first, last = l == 0, l == DEPTH - 1
        mp = dict(mix_w, gain=row(mix_norm[l]), conv_w=conv_w[l], conv_b=row(conv_b[l]),
                  ba=row(lru_ba[l]), bx=row(lru_bx[l]), a_param=row(lru_a_param[l]),
                  pool_b=row(pool_b[l]), pool_scale=row(pool_scale[l]))

        hm = _ffn(hm, row(ffn1_norm[l]), *f1_w, l, rows=meta_rows)
        hx = _ffn(hx, row(ffn1_norm[l]), *f1_w, l, rows=FFN_ROWS, batch_major_in=first)
        hm, meta_state = _mix(hm, mp, l, zero_state, rows=meta_rows, t_offset=0)
        hx, _ = _mix(hx, mp, l, meta_state, rows=MIX_ROWS, t_offset=N_META)
        if not last:
            hm = _ffn(hm, row(ffn2_norm[l]), *f2_w, l, rows=meta_rows)
        hx = _ffn(hx, row(ffn2_norm[l]), *f2_w, l, rows=FFN_ROWS, batch_major_out=last,
                  final_gain=row(final_norm) if last else None)

    return hx
```

```python
import functools

import jax
import jax.numpy as jnp
from jax import lax
from jax.experimental import pallas as pl
from jax.experimental.pallas import tpu as pltpu

D_MODEL = 2048
BATCH = 8
DEPTH = 2
N_META = 16
D_LRU = D_MODEL // 2
LRU_HEADS = 8
LRU_HEAD_DIM = D_LRU // LRU_HEADS
CONV_WIDTH = 4
LRU_C = 8.0
D_POOL = D_MODEL // 2
POOL_WINDOWS = (2, 4, 8, 16)
POOL_GROUP_DIM = D_POOL // len(POOL_WINDOWS)
D_MIX = D_LRU + D_POOL
D_IN = 2 * D_LRU + D_POOL
D_FF = ((8 * D_MODEL // 3 + 255) // 256) * 256
RMS_EPS = 1e-6

SUBLANES = 8
LANES = 128
BF16_TILE_ROWS = 16
CONV_TAIL = (CONV_WIDTH - 1) * BATCH
POOL_TAIL = max(POOL_WINDOWS) * BATCH
V7X_VMEM_LIMIT_BYTES = 60 * 1024 * 1024

FFN_ROWS = 1024
FFN_COLS = 512
MIX_ROWS = 256
SCAN_STEPS = 16
NORM_STEPS = 16

F32 = jnp.float32
BF16 = jnp.bfloat16

assert BATCH == SUBLANES
assert D_FF % FFN_COLS == 0


def _rms_norm(x, gain):
    return x * lax.rsqrt(jnp.mean(x * x, axis=-1, keepdims=True) + RMS_EPS) * gain


def _sigmoid(x):
    return 0.5 * jnp.tanh(0.5 * x) + 0.5


def _gelu_tanh(x):
    return 0.5 * x * (1.0 + jnp.tanh(0.7978845608028654 * (x + 0.044715 * (x * x * x))))


def _ffn_kernel(x_ref, gain_ref, wg_ref, wu_ref, wo_ref, *rest, rows, batch_major_in,
                batch_major_out, final_norm, n_cast):
    rest = list(rest)
    fgain_ref = rest.pop(0) if final_norm else None
    cast_src = [rest.pop(0) for _ in range(n_cast)]
    o_ref = rest.pop(0)
    cast_dst = [rest.pop(0) for _ in range(n_cast)]
    (n_ref,) = rest
    j = pl.program_id(1)
    steps = rows // BATCH

    for src, dst in zip(cast_src, cast_dst):
        dst[...] = src[...].astype(BF16)

    @pl.when(j == 0)
    def _():
        def piece(s, carry):
            t0 = pl.multiple_of(s * NORM_STEPS, NORM_STEPS)
            r0 = pl.multiple_of(s * NORM_STEPS * BATCH, NORM_STEPS * BATCH)
            if batch_major_in:
                x = jnp.swapaxes(x_ref[:, pl.ds(t0, NORM_STEPS), :], 0, 1)
                x = x.reshape(NORM_STEPS * BATCH, D_MODEL)
            else:
                x = x_ref[pl.ds(r0, NORM_STEPS * BATCH), :]
            if batch_major_out:
                x = jnp.swapaxes(x.reshape(NORM_STEPS, BATCH, D_MODEL), 0, 1)
                o_ref[:, pl.ds(t0, NORM_STEPS), :] = x
                n_ref[:, pl.ds(t0, NORM_STEPS), :] = _rms_norm(x, gain_ref[...]).astype(BF16)
            else:
                o_ref[pl.ds(r0, NORM_STEPS * BATCH), :] = x
                n_ref[pl.ds(r0, NORM_STEPS * BATCH), :] = _rms_norm(x, gain_ref[...]).astype(BF16)
            return carry

        lax.fori_loop(0, steps // NORM_STEPS, piece, 0)

    n = n_ref[...].reshape(rows, D_MODEL)
    g = jnp.dot(n, wg_ref[...], preferred_element_type=F32)
    u = jnp.dot(n, wu_ref[...], preferred_element_type=F32)
    act = (g * (0.25 * jnp.tanh(0.5 * g) + 0.25)) * u
    o_ref[...] += jnp.dot(act.astype(BF16), wo_ref[...],
                          preferred_element_type=F32).reshape(o_ref.shape)

    if final_norm:
        @pl.when(j == pl.num_programs(1) - 1)
        def _():
            o_ref[...] = _rms_norm(o_ref[...], fgain_ref[...])


def _cast_slab_specs(w, first_layer, n_tiles, n_chunks):
    layers, r, c = w.shape
    n = layers - first_layer
    assert first_layer % n == 0
    if r % (n_tiles * BF16_TILE_ROWS) == 0 and c % (n_chunks * LANES) == 0:
        block = (n, r // n_tiles, c // n_chunks)
        pos = lambda i, j: (i, j)
    else:
        assert r % (n_tiles * n_chunks * BF16_TILE_ROWS) == 0 and c % LANES == 0
        block = (n, r // (n_tiles * n_chunks), c)
        pos = lambda i, j: (i * n_chunks + j, 0)
    return (pl.BlockSpec(block, lambda i, j: (first_layer // n,) + pos(i, j)),
            pl.BlockSpec(block, lambda i, j: (0,) + pos(i, j)),
            jax.ShapeDtypeStruct((n, r, c), BF16))


def _ffn(h, gain, w_in, w_out, layer, *, rows, batch_major_in=False, batch_major_out=False,
         final_gain=None, cast=()):
    m = h.shape[1] * BATCH if batch_major_in else h.shape[0]
    steps = rows // BATCH
    nf = D_FF // FFN_COLS
    final_norm = final_gain is not None
    row_spec = pl.BlockSpec((rows, D_MODEL), lambda i, j: (i, 0))
    bt_spec = pl.BlockSpec((BATCH, steps, D_MODEL), lambda i, j: (0, i, 0))
    in_specs = [
        bt_spec if batch_major_in else row_spec,
        pl.BlockSpec((1, D_MODEL), lambda i, j: (0, 0)),
        pl.BlockSpec((None, D_MODEL, FFN_COLS), lambda i, j: (layer, 0, j)),
        pl.BlockSpec((None, D_MODEL, FFN_COLS), lambda i, j: (layer, 0, nf + j)),
        pl.BlockSpec((None, FFN_COLS, D_MODEL), lambda i, j: (layer, j, 0)),
    ]
    args = [h, gain, w_in, w_in, w_out]
    if final_norm:
        in_specs.append(pl.BlockSpec((1, D_MODEL), lambda i, j: (0, 0)))
        args.append(final_gain)
    cast_specs = [_cast_slab_specs(w, first, m // rows, nf) for w, first in cast]
    in_specs += [c[0] for c in cast_specs]
    args += [w for w, _ in cast]
    out_shape = (BATCH, m // BATCH, D_MODEL) if batch_major_out else (m, D_MODEL)
    outs = pl.pallas_call(
        functools.partial(_ffn_kernel, rows=rows, batch_major_in=batch_major_in,
                          batch_major_out=batch_major_out, final_norm=final_norm,
                          n_cast=len(cast)),
        grid=(m // rows, nf),
        in_specs=in_specs,
        out_specs=[bt_spec if batch_major_out else row_spec] + [c[1] for c in cast_specs],
        out_shape=[jax.ShapeDtypeStruct(out_shape, F32)] + [c[2] for c in cast_specs],
        scratch_shapes=[pltpu.VMEM((BATCH, steps, D_MODEL) if batch_major_out else (rows, D_MODEL),
                                   BF16)],
        compiler_params=pltpu.CompilerParams(
            dimension_semantics=("parallel", "arbitrary"),
            vmem_limit_bytes=V7X_VMEM_LIMIT_BYTES),
        name="ffn" + ("_from_bt" if batch_major_in else "") + ("_to_bt" if batch_major_out else ""),
    )(*args)
    return (outs[0], *outs[1:]) if cast else outs[0]


def _mix_kernel(x_ref, xprev_ref, gain_ref, win_ref, convw_ref, convb_ref, wgate_ref, ba_ref,
                bx_ref, aparam_ref, poolw_ref, poolb_ref, pools_ref, wout_ref,
                h0_ref, ctail0_ref, ptail0_ref,
                o_ref, hstate_ref, ctail_ref, ptail_ref,
                zx_ext, zp_ext, mix_s, *, rows, t_offset):
    s = pl.program_id(0)
    n_chunks = pl.num_programs(0) - 1
    chunk = jnp.minimum(s, n_chunks - 1)
    steps = rows // BATCH

    @pl.when(s == 0)
    def _():
        hstate_ref[...] = h0_ref[...]
        zx_ext[0:CONV_TAIL, :] = ctail0_ref[...]
        zp_ext[0:POOL_TAIL, :] = ptail0_ref[...]
        mix_s[...] = jnp.zeros_like(mix_s)

    n = _rms_norm(x_ref[...], gain_ref[...]).astype(BF16)
    z = jnp.dot(n, win_ref[...], preferred_element_type=F32)
    zx = z[:, :D_LRU]
    zg = z[:, D_LRU:2 * D_LRU]
    zp = z[:, 2 * D_LRU:]

    zx_ext[CONV_TAIL:CONV_TAIL + rows, :] = zx
    xc = convb_ref[...] + convw_ref[CONV_WIDTH - 1:CONV_WIDTH, :] * zx
    for k in range(CONV_WIDTH - 1):
        xc = xc + convw_ref[k:k + 1, :] * zx_ext[k * BATCH:k * BATCH + rows, :]
    ctail = zx_ext[rows:rows + CONV_TAIL, :]
    zx_ext[0:CONV_TAIL, :] = ctail

    ap = aparam_ref[...]
    neg_c_softplus = -LRU_C * (jnp.maximum(-ap, 0.0) + jnp.log1p(jnp.exp(-jnp.abs(ap))))
    xcb = xc.astype(BF16)
    head_cols = [slice(hd * LRU_HEAD_DIM, (hd + 1) * LRU_HEAD_DIM) for hd in range(LRU_HEADS)]
    gates = [jnp.dot(xcb[:, cols], wgate_ref[hd], preferred_element_type=F32)
             for hd, cols in enumerate(head_cols)]

    zp_ext[POOL_TAIL:POOL_TAIL + rows, :] = zp
    pooled = []
    for gidx, win in enumerate(POOL_WINDOWS):
        cols = slice(gidx * POOL_GROUP_DIM, (gidx + 1) * POOL_GROUP_DIM)
        ext = zp_ext[:, cols]
        acc = ext
        span = 1
        while span < win:
            acc = acc[span * BATCH:, :] + acc[:acc.shape[0] - span * BATCH, :]
            span *= 2
        acc = acc[acc.shape[0] - rows:, :]
        u = ext[POOL_TAIL:, :]
        if t_offset + 1 >= win:
            d = acc * (1.0 / win) - u
        else:
            row = lax.broadcasted_iota(jnp.int32, (rows, POOL_GROUP_DIM), 0)
            t_abs = lax.shift_right_logical(row, 3) + (chunk * steps + t_offset)
            d = acc / jnp.minimum(t_abs + 1, win).astype(F32) - u
        pooled.append(jnp.dot(d.astype(BF16), poolw_ref[gidx], preferred_element_type=F32))
    ptail = zp_ext[rows:rows + POOL_TAIL, :]
    zp_ext[0:POOL_TAIL, :] = ptail

    o_ref[...] = xprev_ref[...] + jnp.dot(mix_s[...], wout_ref[...], preferred_element_type=F32)

    h_prev = hstate_ref[...]
    h_new = []
    mixed = []
    for hd, cols in enumerate(head_cols):
        h = h_prev[:, cols]
        pieces = []
        for r0 in range(0, rows, SCAN_STEPS * BATCH):
            blk = slice(r0, r0 + SCAN_STEPS * BATCH)
            r = _sigmoid(gates[hd][blk, :LRU_HEAD_DIM] + ba_ref[:, cols])
            ig = _sigmoid(gates[hd][blk, LRU_HEAD_DIM:] + bx_ref[:, cols])
            t = jnp.tanh(0.5 * (r * neg_c_softplus[:, cols]))
            q = 1.0 / (1.0 - t)
            a = (1.0 + t) * q
            b = (2.0 * q * jnp.sqrt(-t)) * ig * xc[blk, cols]
            ys = []
            for st in range(SCAN_STEPS):
                rs = slice(st * BATCH, (st + 1) * BATCH)
                h = a[rs, :] * h + b[rs, :]
                ys.append(h)
            pieces.append((jnp.concatenate(ys, axis=0) * _gelu_tanh(zg[blk, cols])).astype(BF16))
        h_new.append(h)
        mixed.append(jnp.concatenate(pieces, axis=0))
    for gidx in range(len(POOL_WINDOWS)):
        cols = slice(gidx * POOL_GROUP_DIM, (gidx + 1) * POOL_GROUP_DIM)
        mixed.append(((pooled[gidx] + poolb_ref[:, cols]) * pools_ref[:, cols]).astype(BF16))
    mix_s[...] = jnp.concatenate(mixed, axis=1)

    @pl.when(s < n_chunks)
    def _():
        hstate_ref[...] = jnp.concatenate(h_new, axis=1)
        ctail_ref[...] = ctail
        ptail_ref[...] = ptail


def _mix(h, p, layer, state, *, rows, t_offset):
    m = h.shape[0]
    n_chunks = m // rows
    const = lambda s: (0, 0)

    def resident(shape):
        return pl.BlockSpec((None,) + shape, lambda s: (layer,) + (0,) * len(shape),
                            pipeline_mode=pl.Buffered(1))

    vec = lambda nl: pl.BlockSpec((1, nl), const)
    cur_spec = pl.BlockSpec((rows, D_MODEL), lambda s: (jnp.minimum(s, n_chunks - 1), 0))
    prev_spec = pl.BlockSpec((rows, D_MODEL), lambda s: (jnp.maximum(s - 1, 0), 0))
    in_specs = [
        cur_spec,
        prev_spec,
        vec(D_MODEL),
        resident((D_MODEL, D_IN)),
        pl.BlockSpec((CONV_WIDTH, D_LRU), const),
        vec(D_LRU),
        resident((LRU_HEADS, LRU_HEAD_DIM, 2 * LRU_HEAD_DIM)),
        vec(D_LRU), vec(D_LRU), vec(D_LRU),
        resident((len(POOL_WINDOWS), POOL_GROUP_DIM, POOL_GROUP_DIM)),
        vec(D_POOL), vec(D_POOL),
        resident((D_MIX, D_MODEL)),
        pl.BlockSpec((BATCH, D_LRU), const),
        pl.BlockSpec((CONV_TAIL, D_LRU), const),
        pl.BlockSpec((POOL_TAIL, D_POOL), const),
    ]
    out_specs = [
        prev_spec,
        pl.BlockSpec((BATCH, D_LRU), const),
        pl.BlockSpec((CONV_TAIL, D_LRU), const),
        pl.BlockSpec((POOL_TAIL, D_POOL), const),
    ]
    out_shape = [
        jax.ShapeDtypeStruct((m, D_MODEL), F32),
        jax.ShapeDtypeStruct((BATCH, D_LRU), F32),
        jax.ShapeDtypeStruct((CONV_TAIL, D_LRU), F32),
        jax.ShapeDtypeStruct((POOL_TAIL, D_POOL), F32),
    ]
    scratch = [
        pltpu.VMEM((rows + CONV_TAIL, D_LRU), F32),
        pltpu.VMEM((rows + POOL_TAIL, D_POOL), F32),
        pltpu.VMEM((rows, D_MIX), BF16),
    ]
    out, hstate, ctail, ptail = pl.pallas_call(
        functools.partial(_mix_kernel, rows=rows, t_offset=t_offset),
        grid=(n_chunks + 1,),
        in_specs=in_specs,
        out_specs=out_specs,
        out_shape=out_shape,
        scratch_shapes=scratch,
        compiler_params=pltpu.CompilerParams(
            dimension_semantics=("arbitrary",),
            vmem_limit_bytes=V7X_VMEM_LIMIT_BYTES),
        name="mix",
    )(h, h, p["gain"], p["w_in"], p["conv_w"], p["conv_b"], p["w_gate"], p["ba"], p["bx"],
      p["a_param"], p["pool_w"], p["pool_b"], p["pool_scale"], p["w_out"], *state)
    return out, (hstate, ctail, ptail)


def kernel(x, meta_tokens, ffn1_norm, ffn1_w_in, ffn1_w_out, mix_norm, w_in, conv_w, conv_b,
           lru_wa, lru_ba, lru_wx, lru_bx, lru_a_param, pool_w, pool_b, pool_scale, w_out,
           ffn2_norm, ffn2_w_in, ffn2_w_out, final_norm):
    b, t, d = x.shape
    assert (b, d) == (BATCH, D_MODEL) and (t * b) % FFN_ROWS == 0 and (t * b) % MIX_ROWS == 0

    hm = jnp.broadcast_to(meta_tokens.astype(x.dtype)[:, None, :], (N_META, b, d)).reshape(N_META * b, d)
    meta_rows = N_META * b
    hx = x

    row = lambda v: v.reshape(1, -1)
    zero_state = (jnp.zeros((BATCH, D_LRU), F32), jnp.zeros((CONV_TAIL, D_LRU), F32),
                  jnp.zeros((POOL_TAIL, D_POOL), F32))
    f1_first = (ffn1_w_in[:1].astype(BF16), ffn1_w_out[:1].astype(BF16))
    mix_w = dict(w_in=w_in.astype(BF16),
                 w_gate=jnp.concatenate([lru_wa, lru_wx], axis=-1).astype(BF16),
                 pool_w=pool_w.astype(BF16), w_out=w_out.astype(BF16))
    hx, f1_in_rest, f1_out_rest, f2_in, f2_out = _ffn(
        hx, row(ffn1_norm[0]), *f1_first, 0, rows=FFN_ROWS, batch_major_in=True,
        cast=((ffn1_w_in, 1), (ffn1_w_out, 1), (ffn2_w_in, 0), (ffn2_w_out, 0)))
    f1_rest = (f1_in_rest, f1_out_rest)
    f2_w = (f2_in, f2_out)

    for l in range(DEPTH):
        last = l == DEPTH - 1
        mp = dict(mix_w, gain=row(mix_norm[l]), conv_w=conv_w[l], conv_b=row(conv_b[l]),
                  ba=row(lru_ba[l]), bx=row(lru_bx[l]), a_param=row(lru_a_param[l]),
                  pool_b=row(pool_b[l]), pool_scale=row(pool_scale[l]))
        f1_w, f1_layer = (f1_first, 0) if l == 0 else (f1_rest, l - 1)

        hm = _ffn(hm, row(ffn1_norm[l]), *f1_w, f1_layer, rows=meta_rows)
        if l > 0:
            hx = _ffn(hx, row(ffn1_norm[l]), *f1_w, f1_layer, rows=FFN_ROWS)
        hm, meta_state = _mix(hm, mp, l, zero_state, rows=meta_rows, t_offset=0)
        hx, _ = _mix(hx, mp, l, meta_state, rows=MIX_ROWS, t_offset=N_META)
        if not last:
            hm = _ffn(hm, row(ffn2_norm[l]), *f2_w, l, rows=meta_rows)
        hx = _ffn(hx, row(ffn2_norm[l]), *f2_w, l, rows=FFN_ROWS, batch_major_out=last,
                  final_gain=row(final_norm) if last else None)

    return hx
```

```python
import functools

import jax
import jax.numpy as jnp
from jax import lax
from jax.experimental import pallas as pl
from jax.experimental.pallas import tpu as pltpu

D_MODEL = 2048
BATCH = 8
DEPTH = 2
N_META = 16
D_LRU = D_MODEL // 2
LRU_HEADS = 8
LRU_HEAD_DIM = D_LRU // LRU_HEADS
CONV_WIDTH = 4
LRU_C = 8.0
D_POOL = D_MODEL // 2
POOL_WINDOWS = (2, 4, 8, 16)
POOL_GROUP_DIM = D_POOL // len(POOL_WINDOWS)
D_MIX = D_LRU + D_POOL
D_IN = 2 * D_LRU + D_POOL
D_FF = ((8 * D_MODEL // 3 + 255) // 256) * 256
RMS_EPS = 1e-6

SUBLANES = 8
LANES = 128
BF16_TILE_ROWS = 16
CONV_TAIL = (CONV_WIDTH - 1) * BATCH
POOL_TAIL = max(POOL_WINDOWS) * BATCH
V7X_VMEM_LIMIT_BYTES = 60 * 1024 * 1024

FFN_ROWS = 1024
FFN_COLS = 512
MIX_ROWS = 256
SCAN_STEPS = 16
NORM_STEPS = 16

F32 = jnp.float32
BF16 = jnp.bfloat16

assert BATCH == SUBLANES
assert D_FF % FFN_COLS == 0


def _rms_norm(x, gain):
    return x * lax.rsqrt(jnp.mean(x * x, axis=-1, keepdims=True) + RMS_EPS) * gain


def _sigmoid(x):
    return 0.5 * jnp.tanh(0.5 * x) + 0.5


def _gelu_tanh(x):
    return 0.5 * x * (1.0 + jnp.tanh(0.7978845608028654 * (x + 0.044715 * (x * x * x))))


def _ffn_kernel(x_ref, gain_ref, wg_ref, wu_ref, wo_ref, *rest, rows, batch_major_in,
                batch_major_out, final_norm, n_cast):
    rest = list(rest)
    fgain_ref = rest.pop(0) if final_norm else None
    cast_src = [rest.pop(0) for _ in range(n_cast)]
    o_ref = rest.pop(0)
    cast_dst = [rest.pop(0) for _ in range(n_cast)]
    (n_ref,) = rest
    j = pl.program_id(1)
    steps = rows // BATCH

    for src, dst in zip(cast_src, cast_dst):
        if len(dst.shape) == 4:
            for q in range(dst.shape[1]):
                dst[:, q] = src[:, :, q * FFN_COLS:(q + 1) * FFN_COLS].astype(BF16)
        else:
            dst[...] = src[...].astype(BF16)

    @pl.when(j == 0)
    def _():
        def piece(s, carry):
            t0 = pl.multiple_of(s * NORM_STEPS, NORM_STEPS)
            r0 = pl.multiple_of(s * NORM_STEPS * BATCH, NORM_STEPS * BATCH)
            if batch_major_in:
                x = jnp.swapaxes(x_ref[:, pl.ds(t0, NORM_STEPS), :], 0, 1)
                x = x.reshape(NORM_STEPS * BATCH, D_MODEL)
            else:
                x = x_ref[pl.ds(r0, NORM_STEPS * BATCH), :]
            if batch_major_out:
                x = jnp.swapaxes(x.reshape(NORM_STEPS, BATCH, D_MODEL), 0, 1)
                o_ref[:, pl.ds(t0, NORM_STEPS), :] = x
                n_ref[:, pl.ds(t0, NORM_STEPS), :] = _rms_norm(x, gain_ref[...]).astype(BF16)
            else:
                o_ref[pl.ds(r0, NORM_STEPS * BATCH), :] = x
                n_ref[pl.ds(r0, NORM_STEPS * BATCH), :] = _rms_norm(x, gain_ref[...]).astype(BF16)
            return carry

        lax.fori_loop(0, steps // NORM_STEPS, piece, 0)

    n = n_ref[...].reshape(rows, D_MODEL)
    g = jnp.dot(n, wg_ref[...], preferred_element_type=F32)
    u = jnp.dot(n, wu_ref[...], preferred_element_type=F32)
    act = (g * (0.25 * jnp.tanh(0.5 * g) + 0.25)) * u
    o_ref[...] += jnp.dot(act.astype(BF16), wo_ref[...],
                          preferred_element_type=F32).reshape(o_ref.shape)

    if final_norm:
        @pl.when(j == pl.num_programs(1) - 1)
        def _():
            o_ref[...] = _rms_norm(o_ref[...], fgain_ref[...])


def _cast_slab_specs(w, first_layer, n_tiles, n_chunks):
    layers, r, c = w.shape
    n = layers - first_layer
    assert first_layer % n == 0
    if r == D_MODEL:
        assert r % (n_tiles * BF16_TILE_ROWS) == 0 and c % (n_chunks * FFN_COLS) == 0
        per_slab = c // n_chunks // FFN_COLS
        return (pl.BlockSpec((n, r // n_tiles, c // n_chunks),
                             lambda i, j: (first_layer // n, i, j)),
                pl.BlockSpec((n, per_slab, r // n_tiles, FFN_COLS), lambda i, j: (0, j, i, 0)),
                jax.ShapeDtypeStruct((n, c // FFN_COLS, r, FFN_COLS), BF16))
    assert r % (n_tiles * n_chunks * BF16_TILE_ROWS) == 0 and c % LANES == 0
    block = (n, r // (n_tiles * n_chunks), c)
    return (pl.BlockSpec(block, lambda i, j: (first_layer // n, i * n_chunks + j, 0)),
            pl.BlockSpec(block, lambda i, j: (0, i * n_chunks + j, 0)),
            jax.ShapeDtypeStruct((n, r, c), BF16))


def _chunk_major(w_in):
    layers, d, c = w_in.shape
    return jnp.transpose(w_in.reshape(layers, d, c // FFN_COLS, FFN_COLS), (0, 2, 1, 3))


def _ffn(h, gain, w_in, w_out, layer, *, rows, batch_major_in=False, batch_major_out=False,
         final_gain=None, cast=()):
    m = h.shape[1] * BATCH if batch_major_in else h.shape[0]
    steps = rows // BATCH
    nf = D_FF // FFN_COLS
    final_norm = final_gain is not None
    row_spec = pl.BlockSpec((rows, D_MODEL), lambda i, j: (i, 0))
    bt_spec = pl.BlockSpec((BATCH, steps, D_MODEL), lambda i, j: (0, i, 0))
    in_specs = [
        bt_spec if batch_major_in else row_spec,
        pl.BlockSpec((1, D_MODEL), lambda i, j: (0, 0)),
        pl.BlockSpec((None, None, D_MODEL, FFN_COLS), lambda i, j: (layer, j, 0, 0)),
        pl.BlockSpec((None, None, D_MODEL, FFN_COLS), lambda i, j: (layer, nf + j, 0, 0)),
        pl.BlockSpec((None, FFN_COLS, D_MODEL), lambda i, j: (layer, j, 0)),
    ]
    args = [h, gain, w_in, w_in, w_out]
    if final_norm:
        in_specs.append(pl.BlockSpec((1, D_MODEL), lambda i, j: (0, 0)))
        args.append(final_gain)
    cast_specs = [_cast_slab_specs(w, first, m // rows, nf) for w, first in cast]
    in_specs += [c[0] for c in cast_specs]
    args += [w for w, _ in cast]
    out_shape = (BATCH, m // BATCH, D_MODEL) if batch_major_out else (m, D_MODEL)
    outs = pl.pallas_call(
        functools.partial(_ffn_kernel, rows=rows, batch_major_in=batch_major_in,
                          batch_major_out=batch_major_out, final_norm=final_norm,
                          n_cast=len(cast)),
        grid=(m // rows, nf),
        in_specs=in_specs,
        out_specs=[bt_spec if batch_major_out else row_spec] + [c[1] for c in cast_specs],
        out_shape=[jax.ShapeDtypeStruct(out_shape, F32)] + [c[2] for c in cast_specs],
        scratch_shapes=[pltpu.VMEM((BATCH, steps, D_MODEL) if batch_major_out else (rows, D_MODEL),
                                   BF16)],
        compiler_params=pltpu.CompilerParams(
            dimension_semantics=("parallel", "arbitrary"),
            vmem_limit_bytes=V7X_VMEM_LIMIT_BYTES),
        name="ffn" + ("_from_bt" if batch_major_in else "") + ("_to_bt" if batch_major_out else ""),
    )(*args)
    return (outs[0], *outs[1:]) if cast else outs[0]


def _mix_kernel(x_ref, xprev_ref, gain_ref, win_ref, convw_ref, convb_ref, wgate_ref, ba_ref,
                bx_ref, aparam_ref, poolw_ref, poolb_ref, pools_ref, wout_ref,
                h0_ref, ctail0_ref, ptail0_ref,
                o_ref, hstate_ref, ctail_ref, ptail_ref,
                zx_ext, zp_ext, mix_s, *, rows, t_offset):
    s = pl.program_id(0)
    n_chunks = pl.num_programs(0) - 1
    chunk = jnp.minimum(s, n_chunks - 1)
    steps = rows // BATCH

    @pl.when(s == 0)
    def _():
        hstate_ref[...] = h0_ref[...]
        zx_ext[0:CONV_TAIL, :] = ctail0_ref[...]
        zp_ext[0:POOL_TAIL, :] = ptail0_ref[...]
        mix_s[...] = jnp.zeros_like(mix_s)

    n = _rms_norm(x_ref[...], gain_ref[...]).astype(BF16)
    z = jnp.dot(n, win_ref[...], preferred_element_type=F32)
    zx = z[:, :D_LRU]
    zg = z[:, D_LRU:2 * D_LRU]
    zp = z[:, 2 * D_LRU:]

    zx_ext[CONV_TAIL:CONV_TAIL + rows, :] = zx
    xc = convb_ref[...] + convw_ref[CONV_WIDTH - 1:CONV_WIDTH, :] * zx
    for k in range(CONV_WIDTH - 1):
        xc = xc + convw_ref[k:k + 1, :] * zx_ext[k * BATCH:k * BATCH + rows, :]
    ctail = zx_ext[rows:rows + CONV_TAIL, :]
    zx_ext[0:CONV_TAIL, :] = ctail

    ap = aparam_ref[...]
    neg_c_softplus = -LRU_C * (jnp.maximum(-ap, 0.0) + jnp.log1p(jnp.exp(-jnp.abs(ap))))
    xcb = xc.astype(BF16)
    head_cols = [slice(hd * LRU_HEAD_DIM, (hd + 1) * LRU_HEAD_DIM) for hd in range(LRU_HEADS)]
    gates = [jnp.dot(xcb[:, cols], wgate_ref[hd], preferred_element_type=F32)
             for hd, cols in enumerate(head_cols)]

    zp_ext[POOL_TAIL:POOL_TAIL + rows, :] = zp
    pooled = []
    for gidx, win in enumerate(POOL_WINDOWS):
        cols = slice(gidx * POOL_GROUP_DIM, (gidx + 1) * POOL_GROUP_DIM)
        ext = zp_ext[:, cols]
        acc = ext
        span = 1
        while span < win:
            acc = acc[span * BATCH:, :] + acc[:acc.shape[0] - span * BATCH, :]
            span *= 2
        acc = acc[acc.shape[0] - rows:, :]
        u = ext[POOL_TAIL:, :]
        if t_offset + 1 >= win:
            d = acc * (1.0 / win) - u
        else:
            row = lax.broadcasted_iota(jnp.int32, (rows, POOL_GROUP_DIM), 0)
            t_abs = lax.shift_right_logical(row, 3) + (chunk * steps + t_offset)
            d = acc / jnp.minimum(t_abs + 1, win).astype(F32) - u
        pooled.append(jnp.dot(d.astype(BF16), poolw_ref[gidx], preferred_element_type=F32))
    ptail = zp_ext[rows:rows + POOL_TAIL, :]
    zp_ext[0:POOL_TAIL, :] = ptail

    o_ref[...] = xprev_ref[...] + jnp.dot(mix_s[...], wout_ref[...], preferred_element_type=F32)

    h_prev = hstate_ref[...]
    h_new = []
    mixed = []
    for hd, cols in enumerate(head_cols):
        h = h_prev[:, cols]
        pieces = []
        for r0 in range(0, rows, SCAN_STEPS * BATCH):
            blk = slice(r0, r0 + SCAN_STEPS * BATCH)
            r = _sigmoid(gates[hd][blk, :LRU_HEAD_DIM] + ba_ref[:, cols])
            ig = _sigmoid(gates[hd][blk, LRU_HEAD_DIM:] + bx_ref[:, cols])
            t = jnp.tanh(0.5 * (r * neg_c_softplus[:, cols]))
            q = 1.0 / (1.0 - t)
            a = (1.0 + t) * q
            b = (2.0 * q * jnp.sqrt(-t)) * ig * xc[blk, cols]
            ys = []
            for st in range(SCAN_STEPS):
                rs = slice(st * BATCH, (st + 1) * BATCH)
                h = a[rs, :] * h + b[rs, :]
                ys.append(h)
            pieces.append((jnp.concatenate(ys, axis=0) * _gelu_tanh(zg[blk, cols])).astype(BF16))
        h_new.append(h)
        mixed.append(jnp.concatenate(pieces, axis=0))
    for gidx in range(len(POOL_WINDOWS)):
        cols = slice(gidx * POOL_GROUP_DIM, (gidx + 1) * POOL_GROUP_DIM)
        mixed.append(((pooled[gidx] + poolb_ref[:, cols]) * pools_ref[:, cols]).astype(BF16))
    mix_s[...] = jnp.concatenate(mixed, axis=1)

    @pl.when(s < n_chunks)
    def _():
        hstate_ref[...] = jnp.concatenate(h_new, axis=1)
        ctail_ref[...] = ctail
        ptail_ref[...] = ptail


def _mix(h, p, layer, state, *, rows, t_offset):
    m = h.shape[0]
    n_chunks = m // rows
    const = lambda s: (0, 0)

    def resident(shape):
        return pl.BlockSpec((None,) + shape, lambda s: (layer,) + (0,) * len(shape),
                            pipeline_mode=pl.Buffered(1))

    vec = lambda nl: pl.BlockSpec((1, nl), const)
    cur_spec = pl.BlockSpec((rows, D_MODEL), lambda s: (jnp.minimum(s, n_chunks - 1), 0))
    prev_spec = pl.BlockSpec((rows, D_MODEL), lambda s: (jnp.maximum(s - 1, 0), 0))
    in_specs = [
        cur_spec,
        prev_spec,
        vec(D_MODEL),
        resident((D_MODEL, D_IN)),
        pl.BlockSpec((CONV_WIDTH, D_LRU), const),
        vec(D_LRU),
        resident((LRU_HEADS, LRU_HEAD_DIM, 2 * LRU_HEAD_DIM)),
        vec(D_LRU), vec(D_LRU), vec(D_LRU),
        resident((len(POOL_WINDOWS), POOL_GROUP_DIM, POOL_GROUP_DIM)),
        vec(D_POOL), vec(D_POOL),
        resident((D_MIX, D_MODEL)),
        pl.BlockSpec((BATCH, D_LRU), const),
        pl.BlockSpec((CONV_TAIL, D_LRU), const),
        pl.BlockSpec((POOL_TAIL, D_POOL), const),
    ]
    out_specs = [
        prev_spec,
        pl.BlockSpec((BATCH, D_LRU), const),
        pl.BlockSpec((CONV_TAIL, D_LRU), const),
        pl.BlockSpec((POOL_TAIL, D_POOL), const),
    ]
    out_shape = [
        jax.ShapeDtypeStruct((m, D_MODEL), F32),
        jax.ShapeDtypeStruct((BATCH, D_LRU), F32),
        jax.ShapeDtypeStruct((CONV_TAIL, D_LRU), F32),
        jax.ShapeDtypeStruct((POOL_TAIL, D_POOL), F32),
    ]
    scratch = [
        pltpu.VMEM((rows + CONV_TAIL, D_LRU), F32),
        pltpu.VMEM((rows + POOL_TAIL, D_POOL), F32),
        pltpu.VMEM((rows, D_MIX), BF16),
    ]
    out, hstate, ctail, ptail = pl.pallas_call(
        functools.partial(_mix_kernel, rows=rows, t_offset=t_offset),
        grid=(n_chunks + 1,),
        in_specs=in_specs,
        out_specs=out_specs,
        out_shape=out_shape,
        scratch_shapes=scratch,
        compiler_params=pltpu.CompilerParams(
            dimension_semantics=("arbitrary",),
            vmem_limit_bytes=V7X_VMEM_LIMIT_BYTES),
        name="mix",
    )(h, h, p["gain"], p["w_in"], p["conv_w"], p["conv_b"], p["w_gate"], p["ba"], p["bx"],
      p["a_param"], p["pool_w"], p["pool_b"], p["pool_scale"], p["w_out"], *state)
    return out, (hstate, ctail, ptail)


def kernel(x, meta_tokens, ffn1_norm, ffn1_w_in, ffn1_w_out, mix_norm, w_in, conv_w, conv_b,
           lru_wa, lru_ba, lru_wx, lru_bx, lru_a_param, pool_w, pool_b, pool_scale, w_out,
           ffn2_norm, ffn2_w_in, ffn2_w_out, final_norm):
    b, t, d = x.shape
    assert (b, d) == (BATCH, D_MODEL) and (t * b) % FFN_ROWS == 0 and (t * b) % MIX_ROWS == 0

    hm = jnp.broadcast_to(meta_tokens.astype(x.dtype)[:, None, :], (N_META, b, d)).reshape(N_META * b, d)
    meta_rows = N_META * b
    hx = x

    row = lambda v: v.reshape(1, -1)
    zero_state = (jnp.zeros((BATCH, D_LRU), F32), jnp.zeros((CONV_TAIL, D_LRU), F32),
                  jnp.zeros((POOL_TAIL, D_POOL), F32))
    f1_first = (_chunk_major(ffn1_w_in[:1]).astype(BF16), ffn1_w_out[:1].astype(BF16))
    mix_w = dict(w_in=w_in.astype(BF16),
                 w_gate=jnp.concatenate([lru_wa, lru_wx], axis=-1).astype(BF16),
                 pool_w=pool_w.astype(BF16), w_out=w_out.astype(BF16))
    hx, f1_in_rest, f1_out_rest, f2_in, f2_out = _ffn(
        hx, row(ffn1_norm[0]), *f1_first, 0, rows=FFN_ROWS, batch_major_in=True,
        cast=((ffn1_w_in, 1), (ffn1_w_out, 1), (ffn2_w_in, 0), (ffn2_w_out, 0)))
    f1_rest = (f1_in_rest, f1_out_rest)
    f2_w = (f2_in, f2_out)

    for l in range(DEPTH):
        last = l == DEPTH - 1
        mp = dict(mix_w, gain=row(mix_norm[l]), conv_w=conv_w[l], conv_b=row(conv_b[l]),
                  ba=row(lru_ba[l]), bx=row(lru_bx[l]), a_param=row(lru_a_param[l]),
                  pool_b=row(pool_b[l]), pool_scale=row(pool_scale[l]))
        f1_w, f1_layer = (f1_first, 0) if l == 0 else (f1_rest, l - 1)

        hm = _ffn(hm, row(ffn1_norm[l]), *f1_w, f1_layer, rows=meta_rows)
        if l > 0:
            hx = _ffn(hx, row(ffn1_norm[l]), *f1_w, f1_layer, rows=FFN_ROWS)
        hm, meta_state = _mix(hm, mp, l, zero_state, rows=meta_rows, t_offset=0)
        hx, _ = _mix(hx, mp, l, meta_state, rows=MIX_ROWS, t_offset=N_META)
        if not last:
            hm = _ffn(hm, row(ffn2_norm[l]), *f2_w, l, rows=meta_rows)
        hx = _ffn(hx, row(ffn2_norm[l]), *f2_w, l, rows=FFN_ROWS, batch_major_out=last,
                  final_gain=row(final_norm) if last else None)

    return hx
```

```python
import functools

import jax
import jax.numpy as jnp
from jax import lax
from jax.experimental import pallas as pl
from jax.experimental.pallas import tpu as pltpu

D_MODEL = 2048
BATCH = 8
DEPTH = 2
N_META = 16
D_LRU = D_MODEL // 2
LRU_HEADS = 8
LRU_HEAD_DIM = D_LRU // LRU_HEADS
CONV_WIDTH = 4
LRU_C = 8.0
D_POOL = D_MODEL // 2
POOL_WINDOWS = (2, 4, 8, 16)
POOL_GROUP_DIM = D_POOL // len(POOL_WINDOWS)
D_MIX = D_LRU + D_POOL
D_IN = 2 * D_LRU + D_POOL
D_FF = ((8 * D_MODEL // 3 + 255) // 256) * 256
RMS_EPS = 1e-6

SUBLANES = 8
LANES = 128
BF16_TILE_ROWS = 16
CONV_TAIL = (CONV_WIDTH - 1) * BATCH
POOL_TAIL = max(POOL_WINDOWS) * BATCH
V7X_VMEM_LIMIT_BYTES = 60 * 1024 * 1024

FFN_ROWS = 1024
FFN_COLS = 512
MIX_ROWS = 256
SCAN_STEPS = 16
NORM_STEPS = 16

F32 = jnp.float32
BF16 = jnp.bfloat16

assert BATCH == SUBLANES
assert D_FF % FFN_COLS == 0


def _rms_norm(x, gain):
    return x * lax.rsqrt(jnp.mean(x * x, axis=-1, keepdims=True) + RMS_EPS) * gain


def _sigmoid(x):
    return 0.5 * jnp.tanh(0.5 * x) + 0.5


def _gelu_tanh(x):
    return 0.5 * x * (1.0 + jnp.tanh(0.7978845608028654 * (x + 0.044715 * (x * x * x))))


def _ffn_kernel(x_ref, gain_ref, wg_ref, wu_ref, wo_ref, *rest, rows, n_tiles, batch_major_in,
                batch_major_out, final_norm, n_cast):
    rest = list(rest)
    fgain_ref = rest.pop(0) if final_norm else None
    cast_src = [rest.pop(0) for _ in range(n_cast)]
    o_ref = rest.pop(0)
    cast_dst = [rest.pop(0) for _ in range(n_cast)]
    n_ref = rest.pop(0)
    xbuf, in_sem = (rest.pop(0), rest.pop(0)) if batch_major_in else (None, None)
    acc, out_sem = (rest.pop(0), rest.pop(0)) if batch_major_out else (None, None)
    assert not rest
    i = pl.program_id(0)
    j = pl.program_id(1)
    last_j = pl.num_programs(1) - 1
    slot = lax.rem(i, 2)
    steps = rows // BATCH

    def fetches(tile, into):
        return [pltpu.make_async_copy(x_ref.at[b, pl.ds(tile * steps, steps), :],
                                      xbuf.at[into, :, b, :], in_sem.at[into])
                for b in range(BATCH)]

    def writebacks(tile, outof):
        return [pltpu.make_async_copy(acc.at[outof, :, b, :],
                                      o_ref.at[b, pl.ds(tile * steps, steps), :], out_sem.at[outof])
                for b in range(BATCH)]

    for src, dst in zip(cast_src, cast_dst):
        dst[...] = src[...].astype(BF16)

    @pl.when(j == 0)
    def _():
        if batch_major_in:
            @pl.when(i == 0)
            def _():
                for c in fetches(0, 0):
                    c.start()

            for c in fetches(i, slot):
                c.wait()

            @pl.when(i + 1 < n_tiles)
            def _():
                for c in fetches(i + 1, 1 - slot):
                    c.start()

        if batch_major_out:
            @pl.when(i >= 2)
            def _():
                for c in writebacks(i - 2, slot):
                    c.wait()

        def piece(s, carry):
            t0 = pl.multiple_of(s * NORM_STEPS, NORM_STEPS)
            r0 = pl.multiple_of(s * NORM_STEPS * BATCH, NORM_STEPS * BATCH)
            if batch_major_in:
                x = xbuf[slot, pl.ds(t0, NORM_STEPS), :, :].reshape(NORM_STEPS * BATCH, D_MODEL)
            else:
                x = x_ref[pl.ds(r0, NORM_STEPS * BATCH), :]
            if batch_major_out:
                acc[slot, pl.ds(t0, NORM_STEPS), :, :] = x.reshape(NORM_STEPS, BATCH, D_MODEL)
            else:
                o_ref[pl.ds(r0, NORM_STEPS * BATCH), :] = x
            n_ref[pl.ds(r0, NORM_STEPS * BATCH), :] = _rms_norm(x, gain_ref[...]).astype(BF16)
            return carry

        lax.fori_loop(0, steps // NORM_STEPS, piece, 0)

    n = n_ref[...]
    g = jnp.dot(n, wg_ref[...], preferred_element_type=F32)
    u = jnp.dot(n, wu_ref[...], preferred_element_type=F32)
    act = (g * (0.25 * jnp.tanh(0.5 * g) + 0.25)) * u
    update = jnp.dot(act.astype(BF16), wo_ref[...], preferred_element_type=F32)
    if batch_major_out:
        acc[slot] += update.reshape(steps, BATCH, D_MODEL)
    else:
        o_ref[...] += update

    if batch_major_out:
        @pl.when(j == last_j)
        def _():
            if final_norm:
                acc[slot] = _rms_norm(acc[slot], fgain_ref[...])
            for c in writebacks(i, slot):
                c.start()

            @pl.when(i == n_tiles - 1)
            def _():
                if n_tiles >= 2:
                    for c in writebacks(i - 1, 1 - slot):
                        c.wait()
                for c in writebacks(i, slot):
                    c.wait()
    elif final_norm:
        @pl.when(j == last_j)
        def _():
            o_ref[...] = _rms_norm(o_ref[...], fgain_ref[...])


def _cast_slab_specs(w, first_layer, n_tiles, n_chunks):
    layers, r, c = w.shape
    n = layers - first_layer
    assert first_layer % n == 0
    if r % (n_tiles * BF16_TILE_ROWS) == 0 and c % (n_chunks * LANES) == 0:
        block = (n, r // n_tiles, c // n_chunks)
        pos = lambda i, j: (i, j)
    else:
        assert r % (n_tiles * n_chunks * BF16_TILE_ROWS) == 0 and c % LANES == 0
        block = (n, r // (n_tiles * n_chunks), c)
        pos = lambda i, j: (i * n_chunks + j, 0)
    return (pl.BlockSpec(block, lambda i, j: (first_layer // n,) + pos(i, j)),
            pl.BlockSpec(block, lambda i, j: (0,) + pos(i, j)),
            jax.ShapeDtypeStruct((n, r, c), BF16))


def _ffn(h, gain, w_in, w_out, layer, *, rows, batch_major_in=False, batch_major_out=False,
         final_gain=None, cast=()):
    m = h.shape[1] * BATCH if batch_major_in else h.shape[0]
    steps = rows // BATCH
    n_tiles = m // rows
    nf = D_FF // FFN_COLS
    final_norm = final_gain is not None
    row_spec = pl.BlockSpec((rows, D_MODEL), lambda i, j: (i, 0))
    hbm_spec = pl.BlockSpec(memory_space=pl.ANY)
    in_specs = [
        hbm_spec if batch_major_in else row_spec,
        pl.BlockSpec((1, D_MODEL), lambda i, j: (0, 0)),
        pl.BlockSpec((None, D_MODEL, FFN_COLS), lambda i, j: (layer, 0, j)),
        pl.BlockSpec((None, D_MODEL, FFN_COLS), lambda i, j: (layer, 0, nf + j)),
        pl.BlockSpec((None, FFN_COLS, D_MODEL), lambda i, j: (layer, j, 0)),
    ]
    args = [h, gain, w_in, w_in, w_out]
    if final_norm:
        in_specs.append(pl.BlockSpec((1, D_MODEL), lambda i, j: (0, 0)))
        args.append(final_gain)
    cast_specs = [_cast_slab_specs(w, first, n_tiles, nf) for w, first in cast]
    in_specs += [c[0] for c in cast_specs]
    args += [w for w, _ in cast]
    out_shape = (BATCH, m // BATCH, D_MODEL) if batch_major_out else (m, D_MODEL)
    scratch = [pltpu.VMEM((rows, D_MODEL), BF16)]
    tile_buffers = [pltpu.VMEM((2, steps, BATCH, D_MODEL), F32), pltpu.SemaphoreType.DMA((2,))]
    if batch_major_in:
        scratch += tile_buffers
    if batch_major_out:
        scratch += tile_buffers
    manual = batch_major_in or batch_major_out
    outs = pl.pallas_call(
        functools.partial(_ffn_kernel, rows=rows, n_tiles=n_tiles, batch_major_in=batch_major_in,
                          batch_major_out=batch_major_out, final_norm=final_norm,
                          n_cast=len(cast)),
        grid=(n_tiles, nf),
        in_specs=in_specs,
        out_specs=[hbm_spec if batch_major_out else row_spec] + [c[1] for c in cast_specs],
        out_shape=[jax.ShapeDtypeStruct(out_shape, F32)] + [c[2] for c in cast_specs],
        scratch_shapes=scratch,
        compiler_params=pltpu.CompilerParams(
            dimension_semantics=("arbitrary" if manual else "parallel", "arbitrary"),
            vmem_limit_bytes=V7X_VMEM_LIMIT_BYTES),
        name="ffn" + ("_from_bt" if batch_major_in else "") + ("_to_bt" if batch_major_out else ""),
    )(*args)
    return (outs[0], *outs[1:]) if cast else outs[0]


def _mix_kernel(x_ref, xprev_ref, gain_ref, win_ref, convw_ref, convb_ref, wgate_ref, ba_ref,
                bx_ref, aparam_ref, poolw_ref, poolb_ref, pools_ref, wout_ref,
                h0_ref, ctail0_ref, ptail0_ref,
                o_ref, hstate_ref, ctail_ref, ptail_ref,
                zx_ext, zp_ext, mix_s, *, rows, t_offset):
    s = pl.program_id(0)
    n_chunks = pl.num_programs(0) - 1
    chunk = jnp.minimum(s, n_chunks - 1)
    steps = rows // BATCH

    @pl.when(s == 0)
    def _():
        hstate_ref[...] = h0_ref[...]
        zx_ext[0:CONV_TAIL, :] = ctail0_ref[...]
        zp_ext[0:POOL_TAIL, :] = ptail0_ref[...]
        mix_s[...] = jnp.zeros_like(mix_s)

    n = _rms_norm(x_ref[...], gain_ref[...]).astype(BF16)
    z = jnp.dot(n, win_ref[...], preferred_element_type=F32)
    zx = z[:, :D_LRU]
    zg = z[:, D_LRU:2 * D_LRU]
    zp = z[:, 2 * D_LRU:]

    zx_ext[CONV_TAIL:CONV_TAIL + rows, :] = zx
    xc = convb_ref[...] + convw_ref[CONV_WIDTH - 1:CONV_WIDTH, :] * zx
    for k in range(CONV_WIDTH - 1):
        xc = xc + convw_ref[k:k + 1, :] * zx_ext[k * BATCH:k * BATCH + rows, :]
    ctail = zx_ext[rows:rows + CONV_TAIL, :]
    zx_ext[0:CONV_TAIL, :] = ctail

    ap = aparam_ref[...]
    neg_c_softplus = -LRU_C * (jnp.maximum(-ap, 0.0) + jnp.log1p(jnp.exp(-jnp.abs(ap))))
    xcb = xc.astype(BF16)
    head_cols = [slice(hd * LRU_HEAD_DIM, (hd + 1) * LRU_HEAD_DIM) for hd in range(LRU_HEADS)]
    gates = [jnp.dot(xcb[:, cols], wgate_ref[hd], preferred_element_type=F32)
             for hd, cols in enumerate(head_cols)]

    zp_ext[POOL_TAIL:POOL_TAIL + rows, :] = zp
    pooled = []
    for gidx, win in enumerate(POOL_WINDOWS):
        cols = slice(gidx * POOL_GROUP_DIM, (gidx + 1) * POOL_GROUP_DIM)
        ext = zp_ext[:, cols]
        acc = ext
        span = 1
        while span < win:
            acc = acc[span * BATCH:, :] + acc[:acc.shape[0] - span * BATCH, :]
            span *= 2
        acc = acc[acc.shape[0] - rows:, :]
        u = ext[POOL_TAIL:, :]
        if t_offset + 1 >= win:
            d = acc * (1.0 / win) - u
        else:
            row = lax.broadcasted_iota(jnp.int32, (rows, POOL_GROUP_DIM), 0)
            t_abs = lax.shift_right_logical(row, 3) + (chunk * steps + t_offset)
            d = acc / jnp.minimum(t_abs + 1, win).astype(F32) - u
        pooled.append(jnp.dot(d.astype(BF16), poolw_ref[gidx], preferred_element_type=F32))
    ptail = zp_ext[rows:rows + POOL_TAIL, :]
    zp_ext[0:POOL_TAIL, :] = ptail

    o_ref[...] = xprev_ref[...] + jnp.dot(mix_s[...], wout_ref[...], preferred_element_type=F32)

    h_prev = hstate_ref[...]
    h_new = []
    mixed = []
    for hd, cols in enumerate(head_cols):
        h = h_prev[:, cols]
        pieces = []
        for r0 in range(0, rows, SCAN_STEPS * BATCH):
            blk = slice(r0, r0 + SCAN_STEPS * BATCH)
            r = _sigmoid(gates[hd][blk, :LRU_HEAD_DIM] + ba_ref[:, cols])
            ig = _sigmoid(gates[hd][blk, LRU_HEAD_DIM:] + bx_ref[:, cols])
            t = jnp.tanh(0.5 * (r * neg_c_softplus[:, cols]))
            q = 1.0 / (1.0 - t)
            a = (1.0 + t) * q
            b = (2.0 * q * jnp.sqrt(-t)) * ig * xc[blk, cols]
            ys = []
            for st in range(SCAN_STEPS):
                rs = slice(st * BATCH, (st + 1) * BATCH)
                h = a[rs, :] * h + b[rs, :]
                ys.append(h)
            pieces.append((jnp.concatenate(ys, axis=0) * _gelu_tanh(zg[blk, cols])).astype(BF16))
        h_new.append(h)
        mixed.append(jnp.concatenate(pieces, axis=0))
    for gidx in range(len(POOL_WINDOWS)):
        cols = slice(gidx * POOL_GROUP_DIM, (gidx + 1) * POOL_GROUP_DIM)
        mixed.append(((pooled[gidx] + poolb_ref[:, cols]) * pools_ref[:, cols]).astype(BF16))
    mix_s[...] = jnp.concatenate(mixed, axis=1)

    @pl.when(s < n_chunks)
    def _():
        hstate_ref[...] = jnp.concatenate(h_new, axis=1)
        ctail_ref[...] = ctail
        ptail_ref[...] = ptail


def _mix(h, p, layer, state, *, rows, t_offset):
    m = h.shape[0]
    n_chunks = m // rows
    const = lambda s: (0, 0)

    def resident(shape):
        return pl.BlockSpec((None,) + shape, lambda s: (layer,) + (0,) * len(shape),
                            pipeline_mode=pl.Buffered(1))

    vec = lambda nl: pl.BlockSpec((1, nl), const)
    cur_spec = pl.BlockSpec((rows, D_MODEL), lambda s: (jnp.minimum(s, n_chunks - 1), 0))
    prev_spec = pl.BlockSpec((rows, D_MODEL), lambda s: (jnp.maximum(s - 1, 0), 0))
    in_specs = [
        cur_spec,
        prev_spec,
        vec(D_MODEL),
        resident((D_MODEL, D_IN)),
        pl.BlockSpec((CONV_WIDTH, D_LRU), const),
        vec(D_LRU),
        resident((LRU_HEADS, LRU_HEAD_DIM, 2 * LRU_HEAD_DIM)),
        vec(D_LRU), vec(D_LRU), vec(D_LRU),
        resident((len(POOL_WINDOWS), POOL_GROUP_DIM, POOL_GROUP_DIM)),
        vec(D_POOL), vec(D_POOL),
        resident((D_MIX, D_MODEL)),
        pl.BlockSpec((BATCH, D_LRU), const),
        pl.BlockSpec((CONV_TAIL, D_LRU), const),
        pl.BlockSpec((POOL_TAIL, D_POOL), const),
    ]
    out_specs = [
        prev_spec,
        pl.BlockSpec((BATCH, D_LRU), const),
        pl.BlockSpec((CONV_TAIL, D_LRU), const),
        pl.BlockSpec((POOL_TAIL, D_POOL), const),
    ]
    out_shape = [
        jax.ShapeDtypeStruct((m, D_MODEL), F32),
        jax.ShapeDtypeStruct((BATCH, D_LRU), F32),
        jax.ShapeDtypeStruct((CONV_TAIL, D_LRU), F32),
        jax.ShapeDtypeStruct((POOL_TAIL, D_POOL), F32),
    ]
    scratch = [
        pltpu.VMEM((rows + CONV_TAIL, D_LRU), F32),
        pltpu.VMEM((rows + POOL_TAIL, D_POOL), F32),
        pltpu.VMEM((rows, D_MIX), BF16),
    ]
    out, hstate, ctail, ptail = pl.pallas_call(
        functools.partial(_mix_kernel, rows=rows, t_offset=t_offset),
        grid=(n_chunks + 1,),
        in_specs=in_specs,
        out_specs=out_specs,
        out_shape=out_shape,
        scratch_shapes=scratch,
        compiler_params=pltpu.CompilerParams(
            dimension_semantics=("arbitrary",),
            vmem_limit_bytes=V7X_VMEM_LIMIT_BYTES),
        name="mix",
    )(h, h, p["gain"], p["w_in"], p["conv_w"], p["conv_b"], p["w_gate"], p["ba"], p["bx"],
      p["a_param"], p["pool_w"], p["pool_b"], p["pool_scale"], p["w_out"], *state)
    return out, (hstate, ctail, ptail)


def kernel(x, meta_tokens, ffn1_norm, ffn1_w_in, ffn1_w_out, mix_norm, w_in, conv_w, conv_b,
           lru_wa, lru_ba, lru_wx, lru_bx, lru_a_param, pool_w, pool_b, pool_scale, w_out,
           ffn2_norm, ffn2_w_in, ffn2_w_out, final_norm):
    b, t, d = x.shape
    assert (b, d) == (BATCH, D_MODEL) and (t * b) % FFN_ROWS == 0 and (t * b) % MIX_ROWS == 0

    hm = jnp.broadcast_to(meta_tokens.astype(x.dtype)[:, None, :], (N_META, b, d)).reshape(N_META * b, d)
    meta_rows = N_META * b
    hx = x

    row = lambda v: v.reshape(1, -1)
    zero_state = (jnp.zeros((BATCH, D_LRU), F32), jnp.zeros((CONV_TAIL, D_LRU), F32),
                  jnp.zeros((POOL_TAIL, D_POOL), F32))
    f1_first = (ffn1_w_in[:1].astype(BF16), ffn1_w_out[:1].astype(BF16))
    mix_w = dict(w_in=w_in.astype(BF16),
                 w_gate=jnp.concatenate([lru_wa, lru_wx], axis=-1).astype(BF16),
                 pool_w=pool_w.astype(BF16), w_out=w_out.astype(BF16))
    hx, f1_in_rest, f1_out_rest, f2_in, f2_out = _ffn(
        hx, row(ffn1_norm[0]), *f1_first, 0, rows=FFN_ROWS, batch_major_in=True,
        cast=((ffn1_w_in, 1), (ffn1_w_out, 1), (ffn2_w_in, 0), (ffn2_w_out, 0)))
    f1_rest = (f1_in_rest, f1_out_rest)
    f2_w = (f2_in, f2_out)

    for l in range(DEPTH):
        last = l == DEPTH - 1
        mp = dict(mix_w, gain=row(mix_norm[l]), conv_w=conv_w[l], conv_b=row(conv_b[l]),
                  ba=row(lru_ba[l]), bx=row(lru_bx[l]), a_param=row(lru_a_param[l]),
                  pool_b=row(pool_b[l]), pool_scale=row(pool_scale[l]))
        f1_w, f1_layer = (f1_first, 0) if l == 0 else (f1_rest, l - 1)

        hm = _ffn(hm, row(ffn1_norm[l]), *f1_w, f1_layer, rows=meta_rows)
        if l > 0:
            hx = _ffn(hx, row(ffn1_norm[l]), *f1_w, f1_layer, rows=FFN_ROWS)
        hm, meta_state = _mix(hm, mp, l, zero_state, rows=meta_rows, t_offset=0)
        hx, _ = _mix(hx, mp, l, meta_state, rows=MIX_ROWS, t_offset=N_META)
        if not last:
            hm = _ffn(hm, row(ffn2_norm[l]), *f2_w, l, rows=meta_rows)
        hx = _ffn(hx, row(ffn2_norm[l]), *f2_w, l, rows=FFN_ROWS, batch_major_out=last,
                  final_gain=row(final_norm) if last else None)

    return hx
```

```python
import functools

import jax
import jax.numpy as jnp
from jax import lax
from jax.experimental import pallas as pl
from jax.experimental.pallas import tpu as pltpu

D_MODEL = 2048
BATCH = 8
DEPTH = 2
N_META = 16
D_LRU = D_MODEL // 2
LRU_HEADS = 8
LRU_HEAD_DIM = D_LRU // LRU_HEADS
CONV_WIDTH = 4
LRU_C = 8.0
D_POOL = D_MODEL // 2
POOL_WINDOWS = (2, 4, 8, 16)
POOL_GROUP_DIM = D_POOL // len(POOL_WINDOWS)
D_MIX = D_LRU + D_POOL
D_IN = 2 * D_LRU + D_POOL
D_FF = ((8 * D_MODEL // 3 + 255) // 256) * 256
RMS_EPS = 1e-6

SUBLANES = 8
LANES = 128
BF16_TILE_ROWS = 16
CONV_TAIL = (CONV_WIDTH - 1) * BATCH
POOL_TAIL = max(POOL_WINDOWS) * BATCH
V7X_VMEM_LIMIT_BYTES = 60 * 1024 * 1024

FFN_ROWS = 1024
FFN_COLS = 512
MIX_ROWS = 512
SCAN_STEPS = 16
NORM_STEPS = 16

F32 = jnp.float32
BF16 = jnp.bfloat16

assert BATCH == SUBLANES
assert D_FF % FFN_COLS == 0


def _rms_norm(x, gain):
    return x * lax.rsqrt(jnp.mean(x * x, axis=-1, keepdims=True) + RMS_EPS) * gain


def _sigmoid(x):
    return 0.5 * jnp.tanh(0.5 * x) + 0.5


def _gelu_tanh(x):
    return 0.5 * x * (1.0 + jnp.tanh(0.7978845608028654 * (x + 0.044715 * (x * x * x))))


def _ffn_kernel(x_ref, gain_ref, wg_ref, wu_ref, wo_ref, *rest, rows, n_tiles, batch_major_in,
                batch_major_out, final_norm, n_cast, f32_weights):
    rest = list(rest)
    fgain_ref = rest.pop(0) if final_norm else None
    cast_src = [rest.pop(0) for _ in range(n_cast)]
    o_ref = rest.pop(0)
    cast_dst = [rest.pop(0) for _ in range(n_cast)]
    bf16_copies = [rest.pop(0) for _ in range(3)] if f32_weights else None
    n_ref = rest.pop(0)
    xbuf, in_sem = (rest.pop(0), rest.pop(0)) if batch_major_in else (None, None)
    acc, out_sem = (rest.pop(0), rest.pop(0)) if batch_major_out else (None, None)
    assert not rest
    i = pl.program_id(0)
    j = pl.program_id(1)
    last_j = pl.num_programs(1) - 1
    slot = lax.rem(i, 2)
    steps = rows // BATCH

    def fetches(tile, into):
        return [pltpu.make_async_copy(x_ref.at[b, pl.ds(tile * steps, steps), :],
                                      xbuf.at[into, :, b, :], in_sem.at[into])
                for b in range(BATCH)]

    def writebacks(tile, outof):
        return [pltpu.make_async_copy(acc.at[outof, :, b, :],
                                      o_ref.at[b, pl.ds(tile * steps, steps), :], out_sem.at[outof])
                for b in range(BATCH)]

    for src, dst in zip(cast_src, cast_dst):
        dst[...] = src[...].astype(BF16)

    @pl.when(j == 0)
    def _():
        if batch_major_in:
            @pl.when(i == 0)
            def _():
                for c in fetches(0, 0):
                    c.start()

            for c in fetches(i, slot):
                c.wait()

            @pl.when(i + 1 < n_tiles)
            def _():
                for c in fetches(i + 1, 1 - slot):
                    c.start()

        if batch_major_out:
            @pl.when(i >= 2)
            def _():
                for c in writebacks(i - 2, slot):
                    c.wait()

        def piece(s, carry):
            t0 = pl.multiple_of(s * NORM_STEPS, NORM_STEPS)
            r0 = pl.multiple_of(s * NORM_STEPS * BATCH, NORM_STEPS * BATCH)
            if batch_major_in:
                x = xbuf[slot, pl.ds(t0, NORM_STEPS), :, :].reshape(NORM_STEPS * BATCH, D_MODEL)
            else:
                x = x_ref[pl.ds(r0, NORM_STEPS * BATCH), :]
            if batch_major_out:
                acc[slot, pl.ds(t0, NORM_STEPS), :, :] = x.reshape(NORM_STEPS, BATCH, D_MODEL)
            else:
                o_ref[pl.ds(r0, NORM_STEPS * BATCH), :] = x
            n_ref[pl.ds(r0, NORM_STEPS * BATCH), :] = _rms_norm(x, gain_ref[...]).astype(BF16)
            return carry

        lax.fori_loop(0, steps // NORM_STEPS, piece, 0)

    wg, wu, wo = wg_ref[...], wu_ref[...], wo_ref[...]
    if f32_weights:
        wg, wu, wo = wg.astype(BF16), wu.astype(BF16), wo.astype(BF16)
        for dst, w in zip(bf16_copies, (wg, wu, wo)):
            dst[0] = w

    n = n_ref[...]
    g = jnp.dot(n, wg, preferred_element_type=F32)
    u = jnp.dot(n, wu, preferred_element_type=F32)
    act = (g * (0.25 * jnp.tanh(0.5 * g) + 0.25)) * u
    update = jnp.dot(act.astype(BF16), wo, preferred_element_type=F32)
    if batch_major_out:
        acc[slot] += update.reshape(steps, BATCH, D_MODEL)
    else:
        o_ref[...] += update

    if batch_major_out:
        @pl.when(j == last_j)
        def _():
            if final_norm:
                acc[slot] = _rms_norm(acc[slot], fgain_ref[...])
            for c in writebacks(i, slot):
                c.start()

            @pl.when(i == n_tiles - 1)
            def _():
                if n_tiles >= 2:
                    for c in writebacks(i - 1, 1 - slot):
                        c.wait()
                for c in writebacks(i, slot):
                    c.wait()
    elif final_norm:
        @pl.when(j == last_j)
        def _():
            o_ref[...] = _rms_norm(o_ref[...], fgain_ref[...])


def _cast_slab_specs(w, first_layer, n_tiles, n_chunks):
    layers, r, c = w.shape
    n = layers - first_layer
    assert first_layer % n == 0
    if r % (n_tiles * BF16_TILE_ROWS) == 0 and c % (n_chunks * LANES) == 0:
        block = (n, r // n_tiles, c // n_chunks)
        pos = lambda i, j: (i, j)
    else:
        assert r % (n_tiles * n_chunks * BF16_TILE_ROWS) == 0 and c % LANES == 0
        block = (n, r // (n_tiles * n_chunks), c)
        pos = lambda i, j: (i * n_chunks + j, 0)
    return (pl.BlockSpec(block, lambda i, j: (first_layer // n,) + pos(i, j)),
            pl.BlockSpec(block, lambda i, j: (0,) + pos(i, j)),
            jax.ShapeDtypeStruct((n, r, c), BF16))


def _ffn(h, gain, weights, layer, *, rows, batch_major_in=False, batch_major_out=False,
         final_gain=None, cast=(), f32_weights=False):
    w_gate, w_up, up_chunk, w_out = weights
    m = h.shape[1] * BATCH if batch_major_in else h.shape[0]
    steps = rows // BATCH
    n_tiles = m // rows
    nf = D_FF // FFN_COLS
    final_norm = final_gain is not None
    row_spec = pl.BlockSpec((rows, D_MODEL), lambda i, j: (i, 0))
    hbm_spec = pl.BlockSpec(memory_space=pl.ANY)
    in_specs = [
        hbm_spec if batch_major_in else row_spec,
        pl.BlockSpec((1, D_MODEL), lambda i, j: (0, 0)),
        pl.BlockSpec((None, D_MODEL, FFN_COLS), lambda i, j: (layer, 0, j)),
        pl.BlockSpec((None, D_MODEL, FFN_COLS), lambda i, j: (layer, 0, up_chunk + j)),
        pl.BlockSpec((None, FFN_COLS, D_MODEL), lambda i, j: (layer, j, 0)),
    ]
    args = [h, gain, w_gate, w_up, w_out]
    if final_norm:
        in_specs.append(pl.BlockSpec((1, D_MODEL), lambda i, j: (0, 0)))
        args.append(final_gain)
    cast_specs = [_cast_slab_specs(w, first, n_tiles, nf) for w, first in cast]
    in_specs += [c[0] for c in cast_specs]
    args += [w for w, _ in cast]
    out_shape = (BATCH, m // BATCH, D_MODEL) if batch_major_out else (m, D_MODEL)
    scratch = [pltpu.VMEM((rows, D_MODEL), BF16)]
    tile_buffers = [pltpu.VMEM((2, steps, BATCH, D_MODEL), F32), pltpu.SemaphoreType.DMA((2,))]
    if batch_major_in:
        scratch += tile_buffers
    if batch_major_out:
        scratch += tile_buffers
    manual = batch_major_in or batch_major_out
    copy_specs, copy_shapes = [], []
    if f32_weights:
        assert n_tiles == 1
        copy_specs = [pl.BlockSpec((1, D_MODEL, FFN_COLS), lambda i, j: (0, 0, j)),
                      pl.BlockSpec((1, D_MODEL, FFN_COLS), lambda i, j: (0, 0, j)),
                      pl.BlockSpec((1, FFN_COLS, D_MODEL), lambda i, j: (0, j, 0))]
        copy_shapes = [jax.ShapeDtypeStruct((1, D_MODEL, D_FF), BF16),
                       jax.ShapeDtypeStruct((1, D_MODEL, D_FF), BF16),
                       jax.ShapeDtypeStruct((1, D_FF, D_MODEL), BF16)]
    outs = pl.pallas_call(
        functools.partial(_ffn_kernel, rows=rows, n_tiles=n_tiles, batch_major_in=batch_major_in,
                          batch_major_out=batch_major_out, final_norm=final_norm,
                          n_cast=len(cast), f32_weights=f32_weights),
        grid=(n_tiles, nf),
        in_specs=in_specs,
        out_specs=([hbm_spec if batch_major_out else row_spec] + [c[1] for c in cast_specs]
                   + copy_specs),
        out_shape=([jax.ShapeDtypeStruct(out_shape, F32)] + [c[2] for c in cast_specs]
                   + copy_shapes),
        scratch_shapes=scratch,
        compiler_params=pltpu.CompilerParams(
            dimension_semantics=("arbitrary" if manual else "parallel", "arbitrary"),
            vmem_limit_bytes=V7X_VMEM_LIMIT_BYTES),
        name="ffn" + ("_from_bt" if batch_major_in else "") + ("_to_bt" if batch_major_out else ""),
    )(*args)
    return tuple(outs) if len(outs) > 1 else outs[0]


def _mix_kernel(x_ref, xprev_ref, gain_ref, win_ref, convw_ref, convb_ref, wgate_ref, ba_ref,
                bx_ref, aparam_ref, poolw_ref, poolb_ref, pools_ref, wout_ref,
                h0_ref, ctail0_ref, ptail0_ref,
                o_ref, hstate_ref, ctail_ref, ptail_ref,
                zx_ext, zp_ext, mix_s, *, rows, t_offset):
    s = pl.program_id(0)
    n_chunks = pl.num_programs(0) - 1
    chunk = jnp.minimum(s, n_chunks - 1)
    steps = rows // BATCH

    @pl.when(s == 0)
    def _():
        hstate_ref[...] = h0_ref[...]
        zx_ext[0:CONV_TAIL, :] = ctail0_ref[...]
        zp_ext[0:POOL_TAIL, :] = ptail0_ref[...]
        mix_s[...] = jnp.zeros_like(mix_s)

    n = _rms_norm(x_ref[...], gain_ref[...]).astype(BF16)
    z = jnp.dot(n, win_ref[...], preferred_element_type=F32)
    zx = z[:, :D_LRU]
    zg = z[:, D_LRU:2 * D_LRU]
    zp = z[:, 2 * D_LRU:]

    zx_ext[CONV_TAIL:CONV_TAIL + rows, :] = zx
    xc = convb_ref[...] + convw_ref[CONV_WIDTH - 1:CONV_WIDTH, :] * zx
    for k in range(CONV_WIDTH - 1):
        xc = xc + convw_ref[k:k + 1, :] * zx_ext[k * BATCH:k * BATCH + rows, :]
    ctail = zx_ext[rows:rows + CONV_TAIL, :]
    zx_ext[0:CONV_TAIL, :] = ctail

    ap = aparam_ref[...]
    neg_c_softplus = -LRU_C * (jnp.maximum(-ap, 0.0) + jnp.log1p(jnp.exp(-jnp.abs(ap))))
    xcb = xc.astype(BF16)
    head_cols = [slice(hd * LRU_HEAD_DIM, (hd + 1) * LRU_HEAD_DIM) for hd in range(LRU_HEADS)]
    gates = [jnp.dot(xcb[:, cols], wgate_ref[hd], preferred_element_type=F32)
             for hd, cols in enumerate(head_cols)]

    zp_ext[POOL_TAIL:POOL_TAIL + rows, :] = zp
    pooled = []
    for gidx, win in enumerate(POOL_WINDOWS):
        cols = slice(gidx * POOL_GROUP_DIM, (gidx + 1) * POOL_GROUP_DIM)
        ext = zp_ext[:, cols]
        acc = ext
        span = 1
        while span < win:
            acc = acc[span * BATCH:, :] + acc[:acc.shape[0] - span * BATCH, :]
            span *= 2
        acc = acc[acc.shape[0] - rows:, :]
        u = ext[POOL_TAIL:, :]
        if t_offset + 1 >= win:
            d = acc * (1.0 / win) - u
        else:
            row = lax.broadcasted_iota(jnp.int32, (rows, POOL_GROUP_DIM), 0)
            t_abs = lax.shift_right_logical(row, 3) + (chunk * steps + t_offset)
            d = acc / jnp.minimum(t_abs + 1, win).astype(F32) - u
        pooled.append(jnp.dot(d.astype(BF16), poolw_ref[gidx], preferred_element_type=F32))
    ptail = zp_ext[rows:rows + POOL_TAIL, :]
    zp_ext[0:POOL_TAIL, :] = ptail

    o_ref[...] = xprev_ref[...] + jnp.dot(mix_s[...], wout_ref[...], preferred_element_type=F32)

    h_prev = hstate_ref[...]
    h_new = []
    mixed = []
    for hd, cols in enumerate(head_cols):
        h = h_prev[:, cols]
        pieces = []
        for r0 in range(0, rows, SCAN_STEPS * BATCH):
            blk = slice(r0, r0 + SCAN_STEPS * BATCH)
            r = _sigmoid(gates[hd][blk, :LRU_HEAD_DIM] + ba_ref[:, cols])
            ig = _sigmoid(gates[hd][blk, LRU_HEAD_DIM:] + bx_ref[:, cols])
            t = jnp.tanh(0.5 * (r * neg_c_softplus[:, cols]))
            q = 1.0 / (1.0 - t)
            a = (1.0 + t) * q
            b = (2.0 * q * jnp.sqrt(-t)) * ig * xc[blk, cols]
            ys = []
            for st in range(SCAN_STEPS):
                rs = slice(st * BATCH, (st + 1) * BATCH)
                h = a[rs, :] * h + b[rs, :]
                ys.append(h)
            pieces.append((jnp.concatenate(ys, axis=0) * _gelu_tanh(zg[blk, cols])).astype(BF16))
        h_new.append(h)
        mixed.append(jnp.concatenate(pieces, axis=0))
    for gidx in range(len(POOL_WINDOWS)):
        cols = slice(gidx * POOL_GROUP_DIM, (gidx + 1) * POOL_GROUP_DIM)
        mixed.append(((pooled[gidx] + poolb_ref[:, cols]) * pools_ref[:, cols]).astype(BF16))
    mix_s[...] = jnp.concatenate(mixed, axis=1)

    @pl.when(s < n_chunks)
    def _():
        hstate_ref[...] = jnp.concatenate(h_new, axis=1)
        ctail_ref[...] = ctail
        ptail_ref[...] = ptail


def _mix(h, p, layer, state, *, rows, t_offset):
    m = h.shape[0]
    n_chunks = m // rows
    const = lambda s: (0, 0)

    def resident(shape):
        return pl.BlockSpec((None,) + shape, lambda s: (layer,) + (0,) * len(shape),
                            pipeline_mode=pl.Buffered(1))

    vec = lambda nl: pl.BlockSpec((1, nl), const)
    cur_spec = pl.BlockSpec((rows, D_MODEL), lambda s: (jnp.minimum(s, n_chunks - 1), 0))
    prev_spec = pl.BlockSpec((rows, D_MODEL), lambda s: (jnp.maximum(s - 1, 0), 0))
    in_specs = [
        cur_spec,
        prev_spec,
        vec(D_MODEL),
        resident((D_MODEL, D_IN)),
        pl.BlockSpec((CONV_WIDTH, D_LRU), const),
        vec(D_LRU),
        resident((LRU_HEADS, LRU_HEAD_DIM, 2 * LRU_HEAD_DIM)),
        vec(D_LRU), vec(D_LRU), vec(D_LRU),
        resident((len(POOL_WINDOWS), POOL_GROUP_DIM, POOL_GROUP_DIM)),
        vec(D_POOL), vec(D_POOL),
        resident((D_MIX, D_MODEL)),
        pl.BlockSpec((BATCH, D_LRU), const),
        pl.BlockSpec((CONV_TAIL, D_LRU), const),
        pl.BlockSpec((POOL_TAIL, D_POOL), const),
    ]
    out_specs = [
        prev_spec,
        pl.BlockSpec((BATCH, D_LRU), const),
        pl.BlockSpec((CONV_TAIL, D_LRU), const),
        pl.BlockSpec((POOL_TAIL, D_POOL), const),
    ]
    out_shape = [
        jax.ShapeDtypeStruct((m, D_MODEL), F32),
        jax.ShapeDtypeStruct((BATCH, D_LRU), F32),
        jax.ShapeDtypeStruct((CONV_TAIL, D_LRU), F32),
        jax.ShapeDtypeStruct((POOL_TAIL, D_POOL), F32),
    ]
    scratch = [
        pltpu.VMEM((rows + CONV_TAIL, D_LRU), F32),
        pltpu.VMEM((rows + POOL_TAIL, D_POOL), F32),
        pltpu.VMEM((rows, D_MIX), BF16),
    ]
    out, hstate, ctail, ptail = pl.pallas_call(
        functools.partial(_mix_kernel, rows=rows, t_offset=t_offset),
        grid=(n_chunks + 1,),
        in_specs=in_specs,
        out_specs=out_specs,
        out_shape=out_shape,
        scratch_shapes=scratch,
        compiler_params=pltpu.CompilerParams(
            dimension_semantics=("arbitrary",),
            vmem_limit_bytes=V7X_VMEM_LIMIT_BYTES),
        name="mix",
    )(h, h, p["gain"], p["w_in"], p["conv_w"], p["conv_b"], p["w_gate"], p["ba"], p["bx"],
      p["a_param"], p["pool_w"], p["pool_b"], p["pool_scale"], p["w_out"], *state)
    return out, (hstate, ctail, ptail)


def kernel(x, meta_tokens, ffn1_norm, ffn1_w_in, ffn1_w_out, mix_norm, w_in, conv_w, conv_b,
           lru_wa, lru_ba, lru_wx, lru_bx, lru_a_param, pool_w, pool_b, pool_scale, w_out,
           ffn2_norm, ffn2_w_in, ffn2_w_out, final_norm):
    b, t, d = x.shape
    assert (b, d) == (BATCH, D_MODEL) and (t * b) % FFN_ROWS == 0 and (t * b) % MIX_ROWS == 0

    hm = jnp.broadcast_to(meta_tokens.astype(x.dtype)[:, None, :], (N_META, b, d)).reshape(N_META * b, d)
    meta_rows = N_META * b
    hx = x

    row = lambda v: v.reshape(1, -1)
    zero_state = (jnp.zeros((BATCH, D_LRU), F32), jnp.zeros((CONV_TAIL, D_LRU), F32),
                  jnp.zeros((POOL_TAIL, D_POOL), F32))
    nf = D_FF // FFN_COLS
    mix_w = dict(w_in=w_in.astype(BF16),
                 w_gate=jnp.concatenate([lru_wa, lru_wx], axis=-1).astype(BF16),
                 pool_w=pool_w.astype(BF16), w_out=w_out.astype(BF16))
    hm, f1_gate, f1_up, f1_out = _ffn(hm, row(ffn1_norm[0]), (ffn1_w_in, ffn1_w_in, nf, ffn1_w_out),
                                      0, rows=meta_rows, f32_weights=True)
    f1_first = (f1_gate, f1_up, 0, f1_out)
    hx, f1_in_rest, f1_out_rest, f2_in, f2_out = _ffn(
        hx, row(ffn1_norm[0]), f1_first, 0, rows=FFN_ROWS, batch_major_in=True,
        cast=((ffn1_w_in, 1), (ffn1_w_out, 1), (ffn2_w_in, 0), (ffn2_w_out, 0)))
    f1_rest = (f1_in_rest, f1_in_rest, nf, f1_out_rest)
    f2_w = (f2_in, f2_in, nf, f2_out)

    for l in range(DEPTH):
        last = l == DEPTH - 1
        mp = dict(mix_w, gain=row(mix_norm[l]), conv_w=conv_w[l], conv_b=row(conv_b[l]),
                  ba=row(lru_ba[l]), bx=row(lru_bx[l]), a_param=row(lru_a_param[l]),
                  pool_b=row(pool_b[l]), pool_scale=row(pool_scale[l]))

        if l > 0:
            hm = _ffn(hm, row(ffn1_norm[l]), f1_rest, l - 1, rows=meta_rows)
            hx = _ffn(hx, row(ffn1_norm[l]), f1_rest, l - 1, rows=FFN_ROWS)
        hm, meta_state = _mix(hm, mp, l, zero_state, rows=meta_rows, t_offset=0)
        hx, _ = _mix(hx, mp, l, meta_state, rows=MIX_ROWS, t_offset=N_META)
        if not last:
            hm = _ffn(hm, row(ffn2_norm[l]), f2_w, l, rows=meta_rows)
        hx = _ffn(hx, row(ffn2_norm[l]), f2_w, l, rows=FFN_ROWS, batch_major_out=last,
                  final_gain=row(final_norm) if last else None)

    return hx
```

```python
import functools

import jax
import jax.numpy as jnp
from jax import lax
from jax.experimental import pallas as pl
from jax.experimental.pallas import tpu as pltpu

D_MODEL = 2048
BATCH = 8
DEPTH = 2
N_META = 16
D_LRU = D_MODEL // 2
LRU_HEADS = 8
LRU_HEAD_DIM = D_LRU // LRU_HEADS
CONV_WIDTH = 4
LRU_C = 8.0
D_POOL = D_MODEL // 2
POOL_WINDOWS = (2, 4, 8, 16)
POOL_GROUP_DIM = D_POOL // len(POOL_WINDOWS)
D_MIX = D_LRU + D_POOL
D_IN = 2 * D_LRU + D_POOL
D_FF = ((8 * D_MODEL // 3 + 255) // 256) * 256
RMS_EPS = 1e-6

SUBLANES = 8
LANES = 128
BF16_TILE_ROWS = 16
CONV_TAIL = (CONV_WIDTH - 1) * BATCH
POOL_TAIL = max(POOL_WINDOWS) * BATCH
V7X_VMEM_LIMIT_BYTES = 60 * 1024 * 1024

FFN_ROWS = 1024
FFN_COLS = 512
MIX_ROWS = 256
SCAN_STEPS = 16
NORM_STEPS = 16

F32 = jnp.float32
BF16 = jnp.bfloat16

assert BATCH == SUBLANES
assert D_FF % FFN_COLS == 0


def _rms_norm(x, gain):
    return x * lax.rsqrt(jnp.mean(x * x, axis=-1, keepdims=True) + RMS_EPS) * gain


def _sigmoid(x):
    return 0.5 * jnp.tanh(0.5 * x) + 0.5


def _gelu_tanh(x):
    return 0.5 * x * (1.0 + jnp.tanh(0.7978845608028654 * (x + 0.044715 * (x * x * x))))


def _zero_word_after(y):
    t = y.reshape(y.shape[0] // SUBLANES, SUBLANES, y.shape[1]).sum(axis=0)
    t = sum(t[:, k:k + LANES] for k in range(0, t.shape[1], LANES))
    half = jnp.uint32(16)
    return lax.shift_right_logical(lax.shift_right_logical(pltpu.bitcast(t, jnp.uint32), half), half)


def _or_into_first_tile(x, word):
    tile_rows = min(x.shape[0], SUBLANES * (4 // x.dtype.itemsize))
    head = pltpu.bitcast(pltpu.bitcast(x[:tile_rows, :LANES], jnp.uint32) | word, x.dtype)
    if x.shape[1] > LANES:
        head = jnp.concatenate([head, x[:tile_rows, LANES:]], axis=1)
    if x.shape[0] > tile_rows:
        head = jnp.concatenate([head, x[tile_rows:, :]], axis=0)
    return head


def _ffn_kernel(x_ref, gain_ref, wg_ref, wu_ref, wo_ref, *rest, rows, n_tiles, batch_major_in,
                batch_major_out, final_norm, n_cast, f32_weights):
    rest = list(rest)
    fgain_ref = rest.pop(0) if final_norm else None
    cast_src = [rest.pop(0) for _ in range(n_cast)]
    o_ref = rest.pop(0)
    cast_dst = [rest.pop(0) for _ in range(n_cast)]
    bf16_copies = [rest.pop(0) for _ in range(3)] if f32_weights else None
    n_ref = rest.pop(0)
    xbuf, in_sem = (rest.pop(0), rest.pop(0)) if batch_major_in else (None, None)
    acc, out_sem = (rest.pop(0), rest.pop(0)) if batch_major_out else (None, None)
    assert not rest
    i = pl.program_id(0)
    j = pl.program_id(1)
    last_j = pl.num_programs(1) - 1
    slot = lax.rem(i, 2)
    steps = rows // BATCH

    def fetches(tile, into):
        return [pltpu.make_async_copy(x_ref.at[b, pl.ds(tile * steps, steps), :],
                                      xbuf.at[into, :, b, :], in_sem.at[into])
                for b in range(BATCH)]

    def writebacks(tile, outof):
        return [pltpu.make_async_copy(acc.at[outof, :, b, :],
                                      o_ref.at[b, pl.ds(tile * steps, steps), :], out_sem.at[outof])
                for b in range(BATCH)]

    for src, dst in zip(cast_src, cast_dst):
        dst[...] = src[...].astype(BF16)

    @pl.when(j == 0)
    def _():
        if batch_major_in:
            @pl.when(i == 0)
            def _():
                for c in fetches(0, 0):
                    c.start()

            for c in fetches(i, slot):
                c.wait()

            @pl.when(i + 1 < n_tiles)
            def _():
                for c in fetches(i + 1, 1 - slot):
                    c.start()

        if batch_major_out:
            @pl.when(i >= 2)
            def _():
                for c in writebacks(i - 2, slot):
                    c.wait()

        def piece(s, carry):
            t0 = pl.multiple_of(s * NORM_STEPS, NORM_STEPS)
            r0 = pl.multiple_of(s * NORM_STEPS * BATCH, NORM_STEPS * BATCH)
            if batch_major_in:
                x = xbuf[slot, pl.ds(t0, NORM_STEPS), :, :].reshape(NORM_STEPS * BATCH, D_MODEL)
            else:
                x = x_ref[pl.ds(r0, NORM_STEPS * BATCH), :]
            if batch_major_out:
                acc[slot, pl.ds(t0, NORM_STEPS), :, :] = x.reshape(NORM_STEPS, BATCH, D_MODEL)
            else:
                o_ref[pl.ds(r0, NORM_STEPS * BATCH), :] = x
            n_ref[pl.ds(r0, NORM_STEPS * BATCH), :] = _rms_norm(x, gain_ref[...]).astype(BF16)
            return carry

        lax.fori_loop(0, steps // NORM_STEPS, piece, 0)

    wg, wu, wo = wg_ref[...], wu_ref[...], wo_ref[...]
    if f32_weights:
        wg, wu, wo = wg.astype(BF16), wu.astype(BF16), wo.astype(BF16)
        for dst, w in zip(bf16_copies, (wg, wu, wo)):
            dst[0] = w

    n = n_ref[...]
    g = jnp.dot(n, wg, preferred_element_type=F32)
    u = jnp.dot(n, wu, preferred_element_type=F32)
    act = (g * (0.25 * jnp.tanh(0.5 * g) + 0.25)) * u
    update = jnp.dot(act.astype(BF16), wo, preferred_element_type=F32)
    if batch_major_out:
        acc[slot] += update.reshape(steps, BATCH, D_MODEL)
    else:
        o_ref[...] += update

    if batch_major_out:
        @pl.when(j == last_j)
        def _():
            if final_norm:
                acc[slot] = _rms_norm(acc[slot], fgain_ref[...])
            for c in writebacks(i, slot):
                c.start()

            @pl.when(i == n_tiles - 1)
            def _():
                if n_tiles >= 2:
                    for c in writebacks(i - 1, 1 - slot):
                        c.wait()
                for c in writebacks(i, slot):
                    c.wait()
    elif final_norm:
        @pl.when(j == last_j)
        def _():
            o_ref[...] = _rms_norm(o_ref[...], fgain_ref[...])


def _cast_slab_specs(w, first_layer, n_tiles, n_chunks):
    layers, r, c = w.shape
    n = layers - first_layer
    assert first_layer % n == 0
    if r % (n_tiles * BF16_TILE_ROWS) == 0 and c % (n_chunks * LANES) == 0:
        block = (n, r // n_tiles, c // n_chunks)
        pos = lambda i, j: (i, j)
    else:
        assert r % (n_tiles * n_chunks * BF16_TILE_ROWS) == 0 and c % LANES == 0
        block = (n, r // (n_tiles * n_chunks), c)
        pos = lambda i, j: (i * n_chunks + j, 0)
    return (pl.BlockSpec(block, lambda i, j: (first_layer // n,) + pos(i, j)),
            pl.BlockSpec(block, lambda i, j: (0,) + pos(i, j)),
            jax.ShapeDtypeStruct((n, r, c), BF16))


def _ffn(h, gain, weights, layer, *, rows, batch_major_in=False, batch_major_out=False,
         final_gain=None, cast=(), f32_weights=False):
    w_gate, w_up, up_chunk, w_out = weights
    m = h.shape[1] * BATCH if batch_major_in else h.shape[0]
    steps = rows // BATCH
    n_tiles = m // rows
    nf = D_FF // FFN_COLS
    final_norm = final_gain is not None
    row_spec = pl.BlockSpec((rows, D_MODEL), lambda i, j: (i, 0))
    hbm_spec = pl.BlockSpec(memory_space=pl.ANY)
    in_specs = [
        hbm_spec if batch_major_in else row_spec,
        pl.BlockSpec((1, D_MODEL), lambda i, j: (0, 0)),
        pl.BlockSpec((None, D_MODEL, FFN_COLS), lambda i, j: (layer, 0, j)),
        pl.BlockSpec((None, D_MODEL, FFN_COLS), lambda i, j: (layer, 0, up_chunk + j)),
        pl.BlockSpec((None, FFN_COLS, D_MODEL), lambda i, j: (layer, j, 0)),
    ]
    args = [h, gain, w_gate, w_up, w_out]
    if final_norm:
        in_specs.append(pl.BlockSpec((1, D_MODEL), lambda i, j: (0, 0)))
        args.append(final_gain)
    cast_specs = [_cast_slab_specs(w, first, n_tiles, nf) for w, first in cast]
    in_specs += [c[0] for c in cast_specs]
    args += [w for w, _ in cast]
    out_shape = (BATCH, m // BATCH, D_MODEL) if batch_major_out else (m, D_MODEL)
    scratch = [pltpu.VMEM((rows, D_MODEL), BF16)]
    tile_buffers = [pltpu.VMEM((2, steps, BATCH, D_MODEL), F32), pltpu.SemaphoreType.DMA((2,))]
    if batch_major_in:
        scratch += tile_buffers
    if batch_major_out:
        scratch += tile_buffers
    manual = batch_major_in or batch_major_out
    copy_specs, copy_shapes = [], []
    if f32_weights:
        assert n_tiles == 1
        copy_specs = [pl.BlockSpec((1, D_MODEL, FFN_COLS), lambda i, j: (0, 0, j)),
                      pl.BlockSpec((1, D_MODEL, FFN_COLS), lambda i, j: (0, 0, j)),
                      pl.BlockSpec((1, FFN_COLS, D_MODEL), lambda i, j: (0, j, 0))]
        copy_shapes = [jax.ShapeDtypeStruct((1, D_MODEL, D_FF), BF16),
                       jax.ShapeDtypeStruct((1, D_MODEL, D_FF), BF16),
                       jax.ShapeDtypeStruct((1, D_FF, D_MODEL), BF16)]
    outs = pl.pallas_call(
        functools.partial(_ffn_kernel, rows=rows, n_tiles=n_tiles, batch_major_in=batch_major_in,
                          batch_major_out=batch_major_out, final_norm=final_norm,
                          n_cast=len(cast), f32_weights=f32_weights),
        grid=(n_tiles, nf),
        in_specs=in_specs,
        out_specs=([hbm_spec if batch_major_out else row_spec] + [c[1] for c in cast_specs]
                   + copy_specs),
        out_shape=([jax.ShapeDtypeStruct(out_shape, F32)] + [c[2] for c in cast_specs]
                   + copy_shapes),
        scratch_shapes=scratch,
        compiler_params=pltpu.CompilerParams(
            dimension_semantics=("arbitrary" if manual else "parallel", "arbitrary"),
            vmem_limit_bytes=V7X_VMEM_LIMIT_BYTES),
        name="ffn" + ("_from_bt" if batch_major_in else "") + ("_to_bt" if batch_major_out else ""),
    )(*args)
    return tuple(outs) if len(outs) > 1 else outs[0]


def _mix_kernel(x_ref, xprev_ref, gain_ref, win_ref, convw_ref, convb_ref, wgate_ref, ba_ref,
                bx_ref, aparam_ref, poolw_ref, poolb_ref, pools_ref, wout_ref,
                h0_ref, ctail0_ref, ptail0_ref,
                o_ref, hstate_ref, ctail_ref, ptail_ref,
                zx_ext, zp_ext, mix_s, *, rows, t_offset):
    s = pl.program_id(0)
    n_chunks = pl.num_programs(0) - 1
    chunk = jnp.minimum(s, n_chunks - 1)
    steps = rows // BATCH

    @pl.when(s == 0)
    def _():
        hstate_ref[...] = h0_ref[...]
        zx_ext[0:CONV_TAIL, :] = ctail0_ref[...]
        zp_ext[0:POOL_TAIL, :] = ptail0_ref[...]
        mix_s[...] = jnp.zeros_like(mix_s)

    n = _rms_norm(x_ref[...], gain_ref[...]).astype(BF16)
    z = jnp.dot(n, win_ref[...], preferred_element_type=F32)
    zx = z[:, :D_LRU]
    zg = z[:, D_LRU:2 * D_LRU]
    zp = z[:, 2 * D_LRU:]

    zx_ext[CONV_TAIL:CONV_TAIL + rows, :] = zx
    xc = convb_ref[...] + convw_ref[CONV_WIDTH - 1:CONV_WIDTH, :] * zx
    for k in range(CONV_WIDTH - 1):
        xc = xc + convw_ref[k:k + 1, :] * zx_ext[k * BATCH:k * BATCH + rows, :]
    ctail = zx_ext[rows:rows + CONV_TAIL, :]
    zx_ext[0:CONV_TAIL, :] = ctail

    ap = aparam_ref[...]
    neg_c_softplus = -LRU_C * (jnp.maximum(-ap, 0.0) + jnp.log1p(jnp.exp(-jnp.abs(ap))))
    xcb = xc.astype(BF16)
    head_cols = [slice(hd * LRU_HEAD_DIM, (hd + 1) * LRU_HEAD_DIM) for hd in range(LRU_HEADS)]
    gates = [jnp.dot(xcb[:, cols], wgate_ref[hd], preferred_element_type=F32)
             for hd, cols in enumerate(head_cols)]

    zp_ext[POOL_TAIL:POOL_TAIL + rows, :] = zp
    pooled = []
    for gidx, win in enumerate(POOL_WINDOWS):
        cols = slice(gidx * POOL_GROUP_DIM, (gidx + 1) * POOL_GROUP_DIM)
        ext = zp_ext[:, cols]
        acc = ext
        span = 1
        while span < win:
            acc = acc[span * BATCH:, :] + acc[:acc.shape[0] - span * BATCH, :]
            span *= 2
        acc = acc[acc.shape[0] - rows:, :]
        u = ext[POOL_TAIL:, :]
        if t_offset + 1 >= win:
            d = acc * (1.0 / win) - u
        else:
            row = lax.broadcasted_iota(jnp.int32, (rows, POOL_GROUP_DIM), 0)
            t_abs = lax.shift_right_logical(row, 3) + (chunk * steps + t_offset)
            d = acc / jnp.minimum(t_abs + 1, win).astype(F32) - u
        pooled.append(jnp.dot(d.astype(BF16), poolw_ref[gidx], preferred_element_type=F32))
    ptail = zp_ext[rows:rows + POOL_TAIL, :]
    zp_ext[0:POOL_TAIL, :] = ptail

    prev_mix = mix_s[...]
    h_prev = hstate_ref[...]
    h_new = []
    mixed = []
    n_parts = rows // (SCAN_STEPS * BATCH)
    part_cols = D_MODEL // n_parts
    k_rows = D_MIX // LRU_HEADS
    outs = [xprev_ref[:, p * part_cols:(p + 1) * part_cols] for p in range(n_parts)]
    elementwise_done = None
    matmul_done = None
    for hd, cols in enumerate(head_cols):
        h = h_prev[:, cols]
        pieces = []
        krows = slice(hd * k_rows, (hd + 1) * k_rows)
        for p in range(n_parts):
            lhs = prev_mix[:, krows]
            if elementwise_done is not None:
                lhs = _or_into_first_tile(lhs, elementwise_done)
            outs[p] = outs[p] + jnp.dot(lhs, wout_ref[krows, p * part_cols:(p + 1) * part_cols],
                                        preferred_element_type=F32)
            ba = ba_ref[:, cols]
            bx = bx_ref[:, cols]
            if matmul_done is not None:
                ba = _or_into_first_tile(ba, matmul_done[:1, :])
                bx = _or_into_first_tile(bx, matmul_done[:1, :])
            blk = slice(p * SCAN_STEPS * BATCH, (p + 1) * SCAN_STEPS * BATCH)
            r = _sigmoid(gates[hd][blk, :LRU_HEAD_DIM] + ba)
            ig = _sigmoid(gates[hd][blk, LRU_HEAD_DIM:] + bx)
            t = jnp.tanh(0.5 * (r * neg_c_softplus[:, cols]))
            q = 1.0 / (1.0 - t)
            a = (1.0 + t) * q
            b = (2.0 * q * jnp.sqrt(-t)) * ig * xc[blk, cols]
            ys = []
            for st in range(SCAN_STEPS):
                rs = slice(st * BATCH, (st + 1) * BATCH)
                h = a[rs, :] * h + b[rs, :]
                ys.append(h)
            y = jnp.concatenate(ys, axis=0) * _gelu_tanh(zg[blk, cols])
            pieces.append(y.astype(BF16))
            elementwise_done = _zero_word_after(y)
            matmul_done = _zero_word_after(outs[p][:, :LANES])
        h_new.append(h)
        mixed.append(jnp.concatenate(pieces, axis=0))
    o_ref[...] = jnp.concatenate(outs, axis=1)
    for gidx in range(len(POOL_WINDOWS)):
        cols = slice(gidx * POOL_GROUP_DIM, (gidx + 1) * POOL_GROUP_DIM)
        mixed.append(((pooled[gidx] + poolb_ref[:, cols]) * pools_ref[:, cols]).astype(BF16))
    mix_s[...] = jnp.concatenate(mixed, axis=1)

    @pl.when(s < n_chunks)
    def _():
        hstate_ref[...] = jnp.concatenate(h_new, axis=1)
        ctail_ref[...] = ctail
        ptail_ref[...] = ptail


def _mix(h, p, layer, state, *, rows, t_offset):
    m = h.shape[0]
    n_chunks = m // rows
    const = lambda s: (0, 0)

    def resident(shape):
        return pl.BlockSpec((None,) + shape, lambda s: (layer,) + (0,) * len(shape),
                            pipeline_mode=pl.Buffered(1))

    vec = lambda nl: pl.BlockSpec((1, nl), const)
    cur_spec = pl.BlockSpec((rows, D_MODEL), lambda s: (jnp.minimum(s, n_chunks - 1), 0))
    prev_spec = pl.BlockSpec((rows, D_MODEL), lambda s: (jnp.maximum(s - 1, 0), 0))
    in_specs = [
        cur_spec,
        prev_spec,
        vec(D_MODEL),
        resident((D_MODEL, D_IN)),
        pl.BlockSpec((CONV_WIDTH, D_LRU), const),
        vec(D_LRU),
        resident((LRU_HEADS, LRU_HEAD_DIM, 2 * LRU_HEAD_DIM)),
        vec(D_LRU), vec(D_LRU), vec(D_LRU),
        resident((len(POOL_WINDOWS), POOL_GROUP_DIM, POOL_GROUP_DIM)),
        vec(D_POOL), vec(D_POOL),
        resident((D_MIX, D_MODEL)),
        pl.BlockSpec((BATCH, D_LRU), const),
        pl.BlockSpec((CONV_TAIL, D_LRU), const),
        pl.BlockSpec((POOL_TAIL, D_POOL), const),
    ]
    out_specs = [
        prev_spec,
        pl.BlockSpec((BATCH, D_LRU), const),
        pl.BlockSpec((CONV_TAIL, D_LRU), const),
        pl.BlockSpec((POOL_TAIL, D_POOL), const),
    ]
    out_shape = [
        jax.ShapeDtypeStruct((m, D_MODEL), F32),
        jax.ShapeDtypeStruct((BATCH, D_LRU), F32),
        jax.ShapeDtypeStruct((CONV_TAIL, D_LRU), F32),
        jax.ShapeDtypeStruct((POOL_TAIL, D_POOL), F32),
    ]
    scratch = [
        pltpu.VMEM((rows + CONV_TAIL, D_LRU), F32),
        pltpu.VMEM((rows + POOL_TAIL, D_POOL), F32),
        pltpu.VMEM((rows, D_MIX), BF16),
    ]
    out, hstate, ctail, ptail = pl.pallas_call(
        functools.partial(_mix_kernel, rows=rows, t_offset=t_offset),
        grid=(n_chunks + 1,),
        in_specs=in_specs,
        out_specs=out_specs,
        out_shape=out_shape,
        scratch_shapes=scratch,
        compiler_params=pltpu.CompilerParams(
            dimension_semantics=("arbitrary",),
            vmem_limit_bytes=V7X_VMEM_LIMIT_BYTES),
        name="mix",
    )(h, h, p["gain"], p["w_in"], p["conv_w"], p["conv_b"], p["w_gate"], p["ba"], p["bx"],
      p["a_param"], p["pool_w"], p["pool_b"], p["pool_scale"], p["w_out"], *state)
    return out, (hstate, ctail, ptail)


def kernel(x, meta_tokens, ffn1_norm, ffn1_w_in, ffn1_w_out, mix_norm, w_in, conv_w, conv_b,
           lru_wa, lru_ba, lru_wx, lru_bx, lru_a_param, pool_w, pool_b, pool_scale, w_out,
           ffn2_norm, ffn2_w_in, ffn2_w_out, final_norm):
    b, t, d = x.shape
    assert (b, d) == (BATCH, D_MODEL) and (t * b) % FFN_ROWS == 0 and (t * b) % MIX_ROWS == 0

    hm = jnp.broadcast_to(meta_tokens.astype(x.dtype)[:, None, :], (N_META, b, d)).reshape(N_META * b, d)
    meta_rows = N_META * b
    hx = x

    row = lambda v: v.reshape(1, -1)
    zero_state = (jnp.zeros((BATCH, D_LRU), F32), jnp.zeros((CONV_TAIL, D_LRU), F32),
                  jnp.zeros((POOL_TAIL, D_POOL), F32))
    nf = D_FF // FFN_COLS
    mix_w = dict(w_in=w_in.astype(BF16),
                 w_gate=jnp.concatenate([lru_wa, lru_wx], axis=-1).astype(BF16),
                 pool_w=pool_w.astype(BF16), w_out=w_out.astype(BF16))
    hm, f1_gate, f1_up, f1_out = _ffn(hm, row(ffn1_norm[0]), (ffn1_w_in, ffn1_w_in, nf, ffn1_w_out),
                                      0, rows=meta_rows, f32_weights=True)
    f1_first = (f1_gate, f1_up, 0, f1_out)
    hx, f1_in_rest, f1_out_rest, f2_in, f2_out = _ffn(
        hx, row(ffn1_norm[0]), f1_first, 0, rows=FFN_ROWS, batch_major_in=True,
        cast=((ffn1_w_in, 1), (ffn1_w_out, 1), (ffn2_w_in, 0), (ffn2_w_out, 0)))
    f1_rest = (f1_in_rest, f1_in_rest, nf, f1_out_rest)
    f2_w = (f2_in, f2_in, nf, f2_out)

    for l in range(DEPTH):
        last = l == DEPTH - 1
        mp = dict(mix_w, gain=row(mix_norm[l]), conv_w=conv_w[l], conv_b=row(conv_b[l]),
                  ba=row(lru_ba[l]), bx=row(lru_bx[l]), a_param=row(lru_a_param[l]),
                  pool_b=row(pool_b[l]), pool_scale=row(pool_scale[l]))

        if l > 0:
            hm = _ffn(hm, row(ffn1_norm[l]), f1_rest, l - 1, rows=meta_rows)
            hx = _ffn(hx, row(ffn1_norm[l]), f1_rest, l - 1, rows=FFN_ROWS)
        hm, meta_state = _mix(hm, mp, l, zero_state, rows=meta_rows, t_offset=0)
        hx, _ = _mix(hx, mp, l, meta_state, rows=MIX_ROWS, t_offset=N_META)
        if not last:
            hm = _ffn(hm, row(ffn2_norm[l]), f2_w, l, rows=meta_rows)
        hx = _ffn(hx, row(ffn2_norm[l]), f2_w, l, rows=FFN_ROWS, batch_major_out=last,
                  final_gain=row(final_norm) if last else None)

    return hx
```

```python
import functools

import jax
import jax.numpy as jnp
from jax import lax
from jax.experimental import pallas as pl
from jax.experimental.pallas import tpu as pltpu

D_MODEL = 2048
BATCH = 8
DEPTH = 2
N_META = 16
D_LRU = D_MODEL // 2
LRU_HEADS = 8
LRU_HEAD_DIM = D_LRU // LRU_HEADS
CONV_WIDTH = 4
LRU_C = 8.0
D_POOL = D_MODEL // 2
POOL_WINDOWS = (2, 4, 8, 16)
POOL_GROUP_DIM = D_POOL // len(POOL_WINDOWS)
D_MIX = D_LRU + D_POOL
D_IN = 2 * D_LRU + D_POOL
D_FF = ((8 * D_MODEL // 3 + 255) // 256) * 256
RMS_EPS = 1e-6

SUBLANES = 8
LANES = 128
BF16_TILE_ROWS = 16
CONV_TAIL = (CONV_WIDTH - 1) * BATCH
POOL_TAIL = max(POOL_WINDOWS) * BATCH
V7X_VMEM_LIMIT_BYTES = 60 * 1024 * 1024
V7X_VMEM_WIDE_LIMIT_BYTES = 62 * 1024 * 1024

FFN_ROWS = 1024
FFN_COLS = 512
MIX_ROWS = 512
SCAN_STEPS = 16
NORM_STEPS = 16

F32 = jnp.float32
BF16 = jnp.bfloat16

assert BATCH == SUBLANES
assert D_FF % FFN_COLS == 0


def _rms_norm(x, gain):
    return x * lax.rsqrt(jnp.mean(x * x, axis=-1, keepdims=True) + RMS_EPS) * gain


def _sigmoid(x):
    return 0.5 * jnp.tanh(0.5 * x) + 0.5


def _gelu_tanh(x):
    return 0.5 * x * (1.0 + jnp.tanh(0.7978845608028654 * (x + 0.044715 * (x * x * x))))


def _ffn_kernel(x_ref, gain_ref, wg_ref, wu_ref, wo_ref, *rest, rows, n_tiles, batch_major_in,
                batch_major_out, final_norm, n_cast, f32_weights):
    rest = list(rest)
    fgain_ref = rest.pop(0) if final_norm else None
    cast_src = [rest.pop(0) for _ in range(n_cast)]
    o_ref = rest.pop(0)
    cast_dst = [rest.pop(0) for _ in range(n_cast)]
    bf16_copies = [rest.pop(0) for _ in range(3)] if f32_weights else None
    n_ref = rest.pop(0)
    xbuf, in_sem = (rest.pop(0), rest.pop(0)) if batch_major_in else (None, None)
    acc, out_sem = (rest.pop(0), rest.pop(0)) if batch_major_out else (None, None)
    assert not rest
    i = pl.program_id(0)
    j = pl.program_id(1)
    last_j = pl.num_programs(1) - 1
    slot = lax.rem(i, 2)
    steps = rows // BATCH

    def fetches(tile, into):
        return [pltpu.make_async_copy(x_ref.at[b, pl.ds(tile * steps, steps), :],
                                      xbuf.at[into, :, b, :], in_sem.at[into])
                for b in range(BATCH)]

    def writebacks(tile, outof):
        return [pltpu.make_async_copy(acc.at[outof, :, b, :],
                                      o_ref.at[b, pl.ds(tile * steps, steps), :], out_sem.at[outof])
                for b in range(BATCH)]

    for src, dst in zip(cast_src, cast_dst):
        dst[...] = src[...].astype(BF16)

    @pl.when(j == 0)
    def _():
        if batch_major_in:
            @pl.when(i == 0)
            def _():
                for c in fetches(0, 0):
                    c.start()

            for c in fetches(i, slot):
                c.wait()

            @pl.when(i + 1 < n_tiles)
            def _():
                for c in fetches(i + 1, 1 - slot):
                    c.start()

        if batch_major_out:
            @pl.when(i >= 2)
            def _():
                for c in writebacks(i - 2, slot):
                    c.wait()

        def piece(s, carry):
            t0 = pl.multiple_of(s * NORM_STEPS, NORM_STEPS)
            r0 = pl.multiple_of(s * NORM_STEPS * BATCH, NORM_STEPS * BATCH)
            if batch_major_in:
                x = xbuf[slot, pl.ds(t0, NORM_STEPS), :, :].reshape(NORM_STEPS * BATCH, D_MODEL)
            else:
                x = x_ref[pl.ds(r0, NORM_STEPS * BATCH), :]
            if batch_major_out:
                acc[slot, pl.ds(t0, NORM_STEPS), :, :] = x.reshape(NORM_STEPS, BATCH, D_MODEL)
            else:
                o_ref[pl.ds(r0, NORM_STEPS * BATCH), :] = x
            n_ref[pl.ds(r0, NORM_STEPS * BATCH), :] = _rms_norm(x, gain_ref[...]).astype(BF16)
            return carry

        lax.fori_loop(0, steps // NORM_STEPS, piece, 0)

    wg, wu, wo = wg_ref[...], wu_ref[...], wo_ref[...]
    if f32_weights:
        wg, wu, wo = wg.astype(BF16), wu.astype(BF16), wo.astype(BF16)
        for dst, w in zip(bf16_copies, (wg, wu, wo)):
            dst[0] = w

    n = n_ref[...]
    g = jnp.dot(n, wg, preferred_element_type=F32)
    u = jnp.dot(n, wu, preferred_element_type=F32)
    act = (g * (0.25 * jnp.tanh(0.5 * g) + 0.25)) * u
    update = jnp.dot(act.astype(BF16), wo, preferred_element_type=F32)
    if batch_major_out:
        acc[slot] += update.reshape(steps, BATCH, D_MODEL)
    else:
        o_ref[...] += update

    if batch_major_out:
        @pl.when(j == last_j)
        def _():
            if final_norm:
                acc[slot] = _rms_norm(acc[slot], fgain_ref[...])
            for c in writebacks(i, slot):
                c.start()

            @pl.when(i == n_tiles - 1)
            def _():
                if n_tiles >= 2:
                    for c in writebacks(i - 1, 1 - slot):
                        c.wait()
                for c in writebacks(i, slot):
                    c.wait()
    elif final_norm:
        @pl.when(j == last_j)
        def _():
            o_ref[...] = _rms_norm(o_ref[...], fgain_ref[...])


def _cast_slab_specs(w, first_layer, n_tiles, n_chunks):
    layers, r, c = w.shape
    n = layers - first_layer
    assert first_layer % n == 0
    if r % (n_tiles * BF16_TILE_ROWS) == 0 and c % (n_chunks * LANES) == 0:
        block = (n, r // n_tiles, c // n_chunks)
        pos = lambda i, j: (i, j)
    else:
        assert r % (n_tiles * n_chunks * BF16_TILE_ROWS) == 0 and c % LANES == 0
        block = (n, r // (n_tiles * n_chunks), c)
        pos = lambda i, j: (i * n_chunks + j, 0)
    return (pl.BlockSpec(block, lambda i, j: (first_layer // n,) + pos(i, j)),
            pl.BlockSpec(block, lambda i, j: (0,) + pos(i, j)),
            jax.ShapeDtypeStruct((n, r, c), BF16))


def _ffn(h, gain, weights, layer, *, rows, batch_major_in=False, batch_major_out=False,
         final_gain=None, cast=(), f32_weights=False):
    w_gate, w_up, up_chunk, w_out = weights
    m = h.shape[1] * BATCH if batch_major_in else h.shape[0]
    steps = rows // BATCH
    n_tiles = m // rows
    nf = D_FF // FFN_COLS
    final_norm = final_gain is not None
    row_spec = pl.BlockSpec((rows, D_MODEL), lambda i, j: (i, 0))
    hbm_spec = pl.BlockSpec(memory_space=pl.ANY)
    in_specs = [
        hbm_spec if batch_major_in else row_spec,
        pl.BlockSpec((1, D_MODEL), lambda i, j: (0, 0)),
        pl.BlockSpec((None, D_MODEL, FFN_COLS), lambda i, j: (layer, 0, j)),
        pl.BlockSpec((None, D_MODEL, FFN_COLS), lambda i, j: (layer, 0, up_chunk + j)),
        pl.BlockSpec((None, FFN_COLS, D_MODEL), lambda i, j: (layer, j, 0)),
    ]
    args = [h, gain, w_gate, w_up, w_out]
    if final_norm:
        in_specs.append(pl.BlockSpec((1, D_MODEL), lambda i, j: (0, 0)))
        args.append(final_gain)
    cast_specs = [_cast_slab_specs(w, first, n_tiles, nf) for w, first in cast]
    in_specs += [c[0] for c in cast_specs]
    args += [w for w, _ in cast]
    out_shape = (BATCH, m // BATCH, D_MODEL) if batch_major_out else (m, D_MODEL)
    scratch = [pltpu.VMEM((rows, D_MODEL), BF16)]
    tile_buffers = [pltpu.VMEM((2, steps, BATCH, D_MODEL), F32), pltpu.SemaphoreType.DMA((2,))]
    if batch_major_in:
        scratch += tile_buffers
    if batch_major_out:
        scratch += tile_buffers
    manual = batch_major_in or batch_major_out
    copy_specs, copy_shapes = [], []
    if f32_weights:
        assert n_tiles == 1
        copy_specs = [pl.BlockSpec((1, D_MODEL, FFN_COLS), lambda i, j: (0, 0, j)),
                      pl.BlockSpec((1, D_MODEL, FFN_COLS), lambda i, j: (0, 0, j)),
                      pl.BlockSpec((1, FFN_COLS, D_MODEL), lambda i, j: (0, j, 0))]
        copy_shapes = [jax.ShapeDtypeStruct((1, D_MODEL, D_FF), BF16),
                       jax.ShapeDtypeStruct((1, D_MODEL, D_FF), BF16),
                       jax.ShapeDtypeStruct((1, D_FF, D_MODEL), BF16)]
    outs = pl.pallas_call(
        functools.partial(_ffn_kernel, rows=rows, n_tiles=n_tiles, batch_major_in=batch_major_in,
                          batch_major_out=batch_major_out, final_norm=final_norm,
                          n_cast=len(cast), f32_weights=f32_weights),
        grid=(n_tiles, nf),
        in_specs=in_specs,
        out_specs=([hbm_spec if batch_major_out else row_spec] + [c[1] for c in cast_specs]
                   + copy_specs),
        out_shape=([jax.ShapeDtypeStruct(out_shape, F32)] + [c[2] for c in cast_specs]
                   + copy_shapes),
        scratch_shapes=scratch,
        compiler_params=pltpu.CompilerParams(
            dimension_semantics=("arbitrary" if manual else "parallel", "arbitrary"),
            vmem_limit_bytes=V7X_VMEM_LIMIT_BYTES),
        name="ffn" + ("_from_bt" if batch_major_in else "") + ("_to_bt" if batch_major_out else ""),
    )(*args)
    return tuple(outs) if len(outs) > 1 else outs[0]


def _ffn_wide_kernel(x_ref, gain_ref, wgate_ref, wup_ref, wout_ref, *rest, rows, n_tiles, n_chunks,
                     layer, up_chunk, batch_major_out, final_norm):
    rest = list(rest)
    fgain_ref = rest.pop(0) if final_norm else None
    o_ref, n_ref, xbuf, x_sem, wg_buf, wu_buf, wo_buf, w_sem = (rest.pop(0) for _ in range(8))
    acc, out_sem = (rest.pop(0), rest.pop(0)) if batch_major_out else (None, None)
    assert not rest
    i = pl.program_id(0)
    j = pl.program_id(1)
    steps = rows // BATCH
    n_steps = (n_chunks + 1) // 2
    last_is_narrow = n_chunks % 2 == 1
    wide_cols = 2 * FFN_COLS
    slot = lax.rem(i * n_steps + j, 2)
    acc_slot = lax.rem(i, 2)

    def x_fetch(tile):
        return pltpu.make_async_copy(x_ref.at[pl.ds(tile * steps, steps)], xbuf, x_sem.at[0])

    def weight_fetches(step, into, cols):
        c0 = step * wide_cols
        if not isinstance(c0, int):
            c0 = pl.multiple_of(c0, wide_cols)
        return [
            pltpu.make_async_copy(wgate_ref.at[layer, :, pl.ds(c0, cols)],
                                  wg_buf.at[into, :, pl.ds(0, cols)], w_sem.at[into]),
            pltpu.make_async_copy(wup_ref.at[layer, :, pl.ds(up_chunk * FFN_COLS + c0, cols)],
                                  wu_buf.at[into, :, pl.ds(0, cols)], w_sem.at[into]),
            pltpu.make_async_copy(wout_ref.at[layer, pl.ds(c0, cols), :],
                                  wo_buf.at[into, pl.ds(0, cols), :], w_sem.at[into]),
        ]

    def writebacks(tile, outof):
        return [pltpu.make_async_copy(acc.at[outof, :, b, :],
                                      o_ref.at[b, pl.ds(tile * steps, steps), :], out_sem.at[outof])
                for b in range(BATCH)]

    def cols_of(step):
        return FFN_COLS if (last_is_narrow and step == n_steps - 1) else wide_cols

    @pl.when(j == 0)
    def _():
        @pl.when(i == 0)
        def _():
            x_fetch(0).start()
            for c in weight_fetches(0, 0, cols_of(0)):
                c.start()

        x_fetch(i).wait()
        if batch_major_out:
            @pl.when(i >= 2)
            def _():
                for c in writebacks(i - 2, acc_slot):
                    c.wait()

        def piece(s, carry):
            t0 = pl.multiple_of(s * NORM_STEPS, NORM_STEPS)
            r0 = pl.multiple_of(s * NORM_STEPS * BATCH, NORM_STEPS * BATCH)
            x = xbuf[pl.ds(t0, NORM_STEPS), :, :]
            if batch_major_out:
                acc[acc_slot, pl.ds(t0, NORM_STEPS), :, :] = x
            x = x.reshape(NORM_STEPS * BATCH, D_MODEL)
            if not batch_major_out:
                o_ref[pl.ds(r0, NORM_STEPS * BATCH), :] = x
            n_ref[pl.ds(r0, NORM_STEPS * BATCH), :] = _rms_norm(x, gain_ref[...]).astype(BF16)
            return carry

        lax.fori_loop(0, steps // NORM_STEPS, piece, 0)

        @pl.when(i + 1 < n_tiles)
        def _():
            x_fetch(i + 1).start()

    def start_fetches(step, cols):
        for c in weight_fetches(step, 1 - slot, cols):
            c.start()

    def step_body(cols):
        for c in weight_fetches(j, slot, cols):
            c.wait()
        nxt = j + 1
        pl.when(nxt < n_steps - 1)(functools.partial(start_fetches, nxt, wide_cols))
        if n_steps > 1:
            pl.when(nxt == n_steps - 1)(functools.partial(start_fetches, nxt, cols_of(n_steps - 1)))
        pl.when((nxt == n_steps) & (i + 1 < n_tiles))(functools.partial(start_fetches, 0, cols_of(0)))

        n = n_ref[...]
        g = jnp.dot(n, wg_buf[slot, :, :cols], preferred_element_type=F32)
        u = jnp.dot(n, wu_buf[slot, :, :cols], preferred_element_type=F32)
        act = (g * (0.25 * jnp.tanh(0.5 * g) + 0.25)) * u
        update = jnp.dot(act.astype(BF16), wo_buf[slot, :cols, :], preferred_element_type=F32)
        if batch_major_out:
            acc[acc_slot] += update.reshape(steps, BATCH, D_MODEL)
        else:
            o_ref[...] += update

    if last_is_narrow:
        pl.when(j < n_steps - 1)(functools.partial(step_body, wide_cols))
        pl.when(j == n_steps - 1)(functools.partial(step_body, FFN_COLS))
    else:
        step_body(wide_cols)

    if batch_major_out:
        @pl.when(j == n_steps - 1)
        def _():
            if final_norm:
                acc[acc_slot] = _rms_norm(acc[acc_slot], fgain_ref[...])
            for c in writebacks(i, acc_slot):
                c.start()

            @pl.when(i == n_tiles - 1)
            def _():
                if n_tiles >= 2:
                    for c in writebacks(i - 1, 1 - acc_slot):
                        c.wait()
                for c in writebacks(i, acc_slot):
                    c.wait()
    elif final_norm:
        @pl.when(j == n_steps - 1)
        def _():
            o_ref[...] = _rms_norm(o_ref[...], fgain_ref[...])


def _ffn_wide(h, gain, weights, layer, *, rows, batch_major_out=False, final_gain=None):
    w_gate, w_up, up_chunk, w_out = weights
    m = h.shape[0]
    steps = rows // BATCH
    n_tiles = m // rows
    nf = D_FF // FFN_COLS
    final_norm = final_gain is not None
    row_spec = pl.BlockSpec((rows, D_MODEL), lambda i, j: (i, 0))
    hbm_spec = pl.BlockSpec(memory_space=pl.ANY)
    in_specs = [hbm_spec, pl.BlockSpec((1, D_MODEL), lambda i, j: (0, 0)), hbm_spec, hbm_spec, hbm_spec]
    args = [h.reshape(m // BATCH, BATCH, D_MODEL), gain, w_gate, w_up, w_out]
    if final_norm:
        in_specs.append(pl.BlockSpec((1, D_MODEL), lambda i, j: (0, 0)))
        args.append(final_gain)
    out_shape = (BATCH, m // BATCH, D_MODEL) if batch_major_out else (m, D_MODEL)
    scratch = [
        pltpu.VMEM((rows, D_MODEL), BF16),
        pltpu.VMEM((steps, BATCH, D_MODEL), F32), pltpu.SemaphoreType.DMA((1,)),
        pltpu.VMEM((2, D_MODEL, 2 * FFN_COLS), BF16),
        pltpu.VMEM((2, D_MODEL, 2 * FFN_COLS), BF16),
        pltpu.VMEM((2, 2 * FFN_COLS, D_MODEL), BF16),
        pltpu.SemaphoreType.DMA((2,)),
    ]
    if batch_major_out:
        scratch += [pltpu.VMEM((2, steps, BATCH, D_MODEL), F32), pltpu.SemaphoreType.DMA((2,))]
    return pl.pallas_call(
        functools.partial(_ffn_wide_kernel, rows=rows, n_tiles=n_tiles, n_chunks=nf, layer=layer,
                          up_chunk=up_chunk, batch_major_out=batch_major_out, final_norm=final_norm),
        grid=(n_tiles, (nf + 1) // 2),
        in_specs=in_specs,
        out_specs=hbm_spec if batch_major_out else row_spec,
        out_shape=jax.ShapeDtypeStruct(out_shape, F32),
        scratch_shapes=scratch,
        compiler_params=pltpu.CompilerParams(
            dimension_semantics=("arbitrary", "arbitrary"),
            vmem_limit_bytes=V7X_VMEM_WIDE_LIMIT_BYTES),
        name="ffn_wide" + ("_to_bt" if batch_major_out else ""),
    )(*args)


def _mix_kernel(x_ref, xprev_ref, gain_ref, win_ref, convw_ref, convb_ref, wgate_ref, ba_ref,
                bx_ref, aparam_ref, poolw_ref, poolb_ref, pools_ref, wout_ref,
                h0_ref, ctail0_ref, ptail0_ref,
                o_ref, hstate_ref, ctail_ref, ptail_ref,
                zx_ext, zp_ext, mix_s, *, rows, t_offset):
    s = pl.program_id(0)
    n_chunks = pl.num_programs(0) - 1
    chunk = jnp.minimum(s, n_chunks - 1)
    steps = rows // BATCH

    @pl.when(s == 0)
    def _():
        hstate_ref[...] = h0_ref[...]
        zx_ext[0:CONV_TAIL, :] = ctail0_ref[...]
        zp_ext[0:POOL_TAIL, :] = ptail0_ref[...]
        mix_s[...] = jnp.zeros_like(mix_s)

    n = _rms_norm(x_ref[...], gain_ref[...]).astype(BF16)
    z = jnp.dot(n, win_ref[...], preferred_element_type=F32)
    zx = z[:, :D_LRU]
    zg = z[:, D_LRU:2 * D_LRU]
    zp = z[:, 2 * D_LRU:]

    zx_ext[CONV_TAIL:CONV_TAIL + rows, :] = zx
    xc = convb_ref[...] + convw_ref[CONV_WIDTH - 1:CONV_WIDTH, :] * zx
    for k in range(CONV_WIDTH - 1):
        xc = xc + convw_ref[k:k + 1, :] * zx_ext[k * BATCH:k * BATCH + rows, :]
    ctail = zx_ext[rows:rows + CONV_TAIL, :]
    zx_ext[0:CONV_TAIL, :] = ctail

    ap = aparam_ref[...]
    neg_c_softplus = -LRU_C * (jnp.maximum(-ap, 0.0) + jnp.log1p(jnp.exp(-jnp.abs(ap))))
    xcb = xc.astype(BF16)
    head_cols = [slice(hd * LRU_HEAD_DIM, (hd + 1) * LRU_HEAD_DIM) for hd in range(LRU_HEADS)]
    gates = [jnp.dot(xcb[:, cols], wgate_ref[hd], preferred_element_type=F32)
             for hd, cols in enumerate(head_cols)]

    zp_ext[POOL_TAIL:POOL_TAIL + rows, :] = zp
    pooled = []
    for gidx, win in enumerate(POOL_WINDOWS):
        cols = slice(gidx * POOL_GROUP_DIM, (gidx + 1) * POOL_GROUP_DIM)
        ext = zp_ext[:, cols]
        acc = ext
        span = 1
        while span < win:
            acc = acc[span * BATCH:, :] + acc[:acc.shape[0] - span * BATCH, :]
            span *= 2
        acc = acc[acc.shape[0] - rows:, :]
        u = ext[POOL_TAIL:, :]
        if t_offset + 1 >= win:
            d = acc * (1.0 / win) - u
        else:
            row = lax.broadcasted_iota(jnp.int32, (rows, POOL_GROUP_DIM), 0)
            t_abs = lax.shift_right_logical(row, 3) + (chunk * steps + t_offset)
            d = acc / jnp.minimum(t_abs + 1, win).astype(F32) - u
        pooled.append(jnp.dot(d.astype(BF16), poolw_ref[gidx], preferred_element_type=F32))
    ptail = zp_ext[rows:rows + POOL_TAIL, :]
    zp_ext[0:POOL_TAIL, :] = ptail

    o_ref[...] = xprev_ref[...] + jnp.dot(mix_s[...], wout_ref[...], preferred_element_type=F32)

    h_prev = hstate_ref[...]
    h_new = []
    mixed = []
    for hd, cols in enumerate(head_cols):
        h = h_prev[:, cols]
        pieces = []
        for r0 in range(0, rows, SCAN_STEPS * BATCH):
            blk = slice(r0, r0 + SCAN_STEPS * BATCH)
            r = _sigmoid(gates[hd][blk, :LRU_HEAD_DIM] + ba_ref[:, cols])
            ig = _sigmoid(gates[hd][blk, LRU_HEAD_DIM:] + bx_ref[:, cols])
            t = jnp.tanh(0.5 * (r * neg_c_softplus[:, cols]))
            q = 1.0 / (1.0 - t)
            a = (1.0 + t) * q
            b = (2.0 * q * jnp.sqrt(-t)) * ig * xc[blk, cols]
            ys = []
            for st in range(SCAN_STEPS):
                rs = slice(st * BATCH, (st + 1) * BATCH)
                h = a[rs, :] * h + b[rs, :]
                ys.append(h)
            pieces.append((jnp.concatenate(ys, axis=0) * _gelu_tanh(zg[blk, cols])).astype(BF16))
        h_new.append(h)
        mixed.append(jnp.concatenate(pieces, axis=0))
    for gidx in range(len(POOL_WINDOWS)):
        cols = slice(gidx * POOL_GROUP_DIM, (gidx + 1) * POOL_GROUP_DIM)
        mixed.append(((pooled[gidx] + poolb_ref[:, cols]) * pools_ref[:, cols]).astype(BF16))
    mix_s[...] = jnp.concatenate(mixed, axis=1)

    @pl.when(s < n_chunks)
    def _():
        hstate_ref[...] = jnp.concatenate(h_new, axis=1)
        ctail_ref[...] = ctail
        ptail_ref[...] = ptail


def _mix(h, p, layer, state, *, rows, t_offset):
    m = h.shape[0]
    n_chunks = m // rows
    const = lambda s: (0, 0)

    def resident(shape):
        return pl.BlockSpec((None,) + shape, lambda s: (layer,) + (0,) * len(shape),
                            pipeline_mode=pl.Buffered(1))

    vec = lambda nl: pl.BlockSpec((1, nl), const)
    cur_spec = pl.BlockSpec((rows, D_MODEL), lambda s: (jnp.minimum(s, n_chunks - 1), 0))
    prev_spec = pl.BlockSpec((rows, D_MODEL), lambda s: (jnp.maximum(s - 1, 0), 0))
    in_specs = [
        cur_spec,
        prev_spec,
        vec(D_MODEL),
        resident((D_MODEL, D_IN)),
        pl.BlockSpec((CONV_WIDTH, D_LRU), const),
        vec(D_LRU),
        resident((LRU_HEADS, LRU_HEAD_DIM, 2 * LRU_HEAD_DIM)),
        vec(D_LRU), vec(D_LRU), vec(D_LRU),
        resident((len(POOL_WINDOWS), POOL_GROUP_DIM, POOL_GROUP_DIM)),
        vec(D_POOL), vec(D_POOL),
        resident((D_MIX, D_MODEL)),
        pl.BlockSpec((BATCH, D_LRU), const),
        pl.BlockSpec((CONV_TAIL, D_LRU), const),
        pl.BlockSpec((POOL_TAIL, D_POOL), const),
    ]
    out_specs = [
        prev_spec,
        pl.BlockSpec((BATCH, D_LRU), const),
        pl.BlockSpec((CONV_TAIL, D_LRU), const),
        pl.BlockSpec((POOL_TAIL, D_POOL), const),
    ]
    out_shape = [
        jax.ShapeDtypeStruct((m, D_MODEL), F32),
        jax.ShapeDtypeStruct((BATCH, D_LRU), F32),
        jax.ShapeDtypeStruct((CONV_TAIL, D_LRU), F32),
        jax.ShapeDtypeStruct((POOL_TAIL, D_POOL), F32),
    ]
    scratch = [
        pltpu.VMEM((rows + CONV_TAIL, D_LRU), F32),
        pltpu.VMEM((rows + POOL_TAIL, D_POOL), F32),
        pltpu.VMEM((rows, D_MIX), BF16),
    ]
    out, hstate, ctail, ptail = pl.pallas_call(
        functools.partial(_mix_kernel, rows=rows, t_offset=t_offset),
        grid=(n_chunks + 1,),
        in_specs=in_specs,
        out_specs=out_specs,
        out_shape=out_shape,
        scratch_shapes=scratch,
        compiler_params=pltpu.CompilerParams(
            dimension_semantics=("arbitrary",),
            vmem_limit_bytes=V7X_VMEM_LIMIT_BYTES),
        name="mix",
    )(h, h, p["gain"], p["w_in"], p["conv_w"], p["conv_b"], p["w_gate"], p["ba"], p["bx"],
      p["a_param"], p["pool_w"], p["pool_b"], p["pool_scale"], p["w_out"], *state)
    return out, (hstate, ctail, ptail)


def kernel(x, meta_tokens, ffn1_norm, ffn1_w_in, ffn1_w_out, mix_norm, w_in, conv_w, conv_b,
           lru_wa, lru_ba, lru_wx, lru_bx, lru_a_param, pool_w, pool_b, pool_scale, w_out,
           ffn2_norm, ffn2_w_in, ffn2_w_out, final_norm):
    b, t, d = x.shape
    assert (b, d) == (BATCH, D_MODEL) and (t * b) % FFN_ROWS == 0 and (t * b) % MIX_ROWS == 0

    hm = jnp.broadcast_to(meta_tokens.astype(x.dtype)[:, None, :], (N_META, b, d)).reshape(N_META * b, d)
    meta_rows = N_META * b
    hx = x

    row = lambda v: v.reshape(1, -1)
    zero_state = (jnp.zeros((BATCH, D_LRU), F32), jnp.zeros((CONV_TAIL, D_LRU), F32),
                  jnp.zeros((POOL_TAIL, D_POOL), F32))
    nf = D_FF // FFN_COLS
    mix_w = dict(w_in=w_in.astype(BF16),
                 w_gate=jnp.concatenate([lru_wa, lru_wx], axis=-1).astype(BF16),
                 pool_w=pool_w.astype(BF16), w_out=w_out.astype(BF16))
    hm, f1_gate, f1_up, f1_out = _ffn(hm, row(ffn1_norm[0]), (ffn1_w_in, ffn1_w_in, nf, ffn1_w_out),
                                      0, rows=meta_rows, f32_weights=True)
    f1_first = (f1_gate, f1_up, 0, f1_out)
    hx, f1_in_rest, f1_out_rest, f2_in, f2_out = _ffn(
        hx, row(ffn1_norm[0]), f1_first, 0, rows=FFN_ROWS, batch_major_in=True,
        cast=((ffn1_w_in, 1), (ffn1_w_out, 1), (ffn2_w_in, 0), (ffn2_w_out, 0)))
    f1_rest = (f1_in_rest, f1_in_rest, nf, f1_out_rest)
    f2_w = (f2_in, f2_in, nf, f2_out)

    for l in range(DEPTH):
        last = l == DEPTH - 1
        mp = dict(mix_w, gain=row(mix_norm[l]), conv_w=conv_w[l], conv_b=row(conv_b[l]),
                  ba=row(lru_ba[l]), bx=row(lru_bx[l]), a_param=row(lru_a_param[l]),
                  pool_b=row(pool_b[l]), pool_scale=row(pool_scale[l]))

        if l > 0:
            hm = _ffn(hm, row(ffn1_norm[l]), f1_rest, l - 1, rows=meta_rows)
            hx = _ffn_wide(hx, row(ffn1_norm[l]), f1_rest, l - 1, rows=FFN_ROWS)
        hm, meta_state = _mix(hm, mp, l, zero_state, rows=meta_rows, t_offset=0)
        hx, _ = _mix(hx, mp, l, meta_state, rows=MIX_ROWS, t_offset=N_META)
        if not last:
            hm = _ffn(hm, row(ffn2_norm[l]), f2_w, l, rows=meta_rows)
        hx = _ffn_wide(hx, row(ffn2_norm[l]), f2_w, l, rows=FFN_ROWS, batch_major_out=last,
                       final_gain=row(final_norm) if last else None)

    return hx
```

```python
import functools

import jax
import jax.numpy as jnp
from jax import lax
from jax.experimental import pallas as pl
from jax.experimental.pallas import tpu as pltpu

D_MODEL = 2048
BATCH = 8
DEPTH = 2
N_META = 16
D_LRU = D_MODEL // 2
LRU_HEADS = 8
LRU_HEAD_DIM = D_LRU // LRU_HEADS
CONV_WIDTH = 4
LRU_C = 8.0
D_POOL = D_MODEL // 2
POOL_WINDOWS = (2, 4, 8, 16)
POOL_GROUP_DIM = D_POOL // len(POOL_WINDOWS)
D_MIX = D_LRU + D_POOL
D_IN = 2 * D_LRU + D_POOL
D_FF = ((8 * D_MODEL // 3 + 255) // 256) * 256
RMS_EPS = 1e-6

SUBLANES = 8
LANES = 128
BF16_TILE_ROWS = 16
CONV_TAIL = (CONV_WIDTH - 1) * BATCH
POOL_TAIL = max(POOL_WINDOWS) * BATCH
V7X_VMEM_LIMIT_BYTES = 60 * 1024 * 1024

FFN_ROWS = 1024
FFN_COLS = 512
MIX_ROWS = 512
SCAN_STEPS = 16
NORM_STEPS = 32

F32 = jnp.float32
BF16 = jnp.bfloat16

assert BATCH == SUBLANES
assert D_FF % FFN_COLS == 0


def _rms_norm(x, gain):
    return x * lax.rsqrt(jnp.mean(x * x, axis=-1, keepdims=True) + RMS_EPS) * gain


def _sigmoid(x):
    return 0.5 * jnp.tanh(0.5 * x) + 0.5


def _gelu_tanh(x):
    return 0.5 * x * (1.0 + jnp.tanh(0.7978845608028654 * (x + 0.044715 * (x * x * x))))


def _ffn_kernel(x_ref, gain_ref, wg_ref, wu_ref, wo_ref, *rest, rows, n_tiles, batch_major_in,
                batch_major_out, final_norm, n_cast, f32_weights):
    rest = list(rest)
    fgain_ref = rest.pop(0) if final_norm else None
    cast_src = [rest.pop(0) for _ in range(n_cast)]
    o_ref = rest.pop(0)
    cast_dst = [rest.pop(0) for _ in range(n_cast)]
    bf16_copies = [rest.pop(0) for _ in range(3)] if f32_weights else None
    n_ref = rest.pop(0)
    xbuf, in_sem = (rest.pop(0), rest.pop(0)) if batch_major_in else (None, None)
    acc, out_sem = (rest.pop(0), rest.pop(0)) if batch_major_out else (None, None)
    assert not rest
    i = pl.program_id(0)
    j = pl.program_id(1)
    last_j = pl.num_programs(1) - 1
    slot = lax.rem(i, 2)
    steps = rows // BATCH

    def fetches(tile, into):
        return [pltpu.make_async_copy(x_ref.at[b, pl.ds(tile * steps, steps), :],
                                      xbuf.at[into, :, b, :], in_sem.at[into])
                for b in range(BATCH)]

    def writebacks(tile, outof):
        return [pltpu.make_async_copy(acc.at[outof, :, b, :],
                                      o_ref.at[b, pl.ds(tile * steps, steps), :], out_sem.at[outof])
                for b in range(BATCH)]

    for src, dst in zip(cast_src, cast_dst):
        dst[...] = src[...].astype(BF16)

    @pl.when(j == 0)
    def _():
        if batch_major_in:
            @pl.when(i == 0)
            def _():
                for c in fetches(0, 0):
                    c.start()

            for c in fetches(i, slot):
                c.wait()

            @pl.when(i + 1 < n_tiles)
            def _():
                for c in fetches(i + 1, 1 - slot):
                    c.start()

        if batch_major_out:
            @pl.when(i >= 2)
            def _():
                for c in writebacks(i - 2, slot):
                    c.wait()

        pst = min(NORM_STEPS, steps)

        def piece(s, carry):
            t0 = pl.multiple_of(s * pst, pst)
            r0 = pl.multiple_of(s * pst * BATCH, pst * BATCH)
            if batch_major_in:
                x = xbuf[slot, pl.ds(t0, pst), :, :].reshape(pst * BATCH, D_MODEL)
            else:
                x = x_ref[pl.ds(r0, pst * BATCH), :]
            if batch_major_out:
                acc[slot, pl.ds(t0, pst), :, :] = x.reshape(pst, BATCH, D_MODEL)
            else:
                o_ref[pl.ds(r0, pst * BATCH), :] = x
            n_ref[pl.ds(r0, pst * BATCH), :] = _rms_norm(x, gain_ref[...]).astype(BF16)
            return carry

        lax.fori_loop(0, steps // pst, piece, 0)

    wg, wu, wo = wg_ref[...], wu_ref[...], wo_ref[...]
    if f32_weights:
        wg, wu, wo = wg.astype(BF16), wu.astype(BF16), wo.astype(BF16)
        for dst, w in zip(bf16_copies, (wg, wu, wo)):
            dst[0] = w

    n = n_ref[...]
    g = jnp.dot(n, wg, preferred_element_type=F32)
    u = jnp.dot(n, wu, preferred_element_type=F32)
    act = (g * (0.25 * jnp.tanh(0.5 * g) + 0.25)) * u
    update = jnp.dot(act.astype(BF16), wo, preferred_element_type=F32)
    if batch_major_out:
        acc[slot] += update.reshape(steps, BATCH, D_MODEL)
    else:
        o_ref[...] += update

    if batch_major_out:
        @pl.when(j == last_j)
        def _():
            if final_norm:
                acc[slot] = _rms_norm(acc[slot], fgain_ref[...])
            for c in writebacks(i, slot):
                c.start()

            @pl.when(i == n_tiles - 1)
            def _():
                if n_tiles >= 2:
                    for c in writebacks(i - 1, 1 - slot):
                        c.wait()
                for c in writebacks(i, slot):
                    c.wait()
    elif final_norm:
        @pl.when(j == last_j)
        def _():
            o_ref[...] = _rms_norm(o_ref[...], fgain_ref[...])


def _cast_slab_specs(w, first_layer, n_tiles, n_chunks):
    layers, r, c = w.shape
    n = layers - first_layer
    assert first_layer % n == 0
    if r % (n_tiles * BF16_TILE_ROWS) == 0 and c % (n_chunks * LANES) == 0:
        block = (n, r // n_tiles, c // n_chunks)
        pos = lambda i, j: (i, j)
    else:
        assert r % (n_tiles * n_chunks * BF16_TILE_ROWS) == 0 and c % LANES == 0
        block = (n, r // (n_tiles * n_chunks), c)
        pos = lambda i, j: (i * n_chunks + j, 0)
    return (pl.BlockSpec(block, lambda i, j: (first_layer // n,) + pos(i, j)),
            pl.BlockSpec(block, lambda i, j: (0,) + pos(i, j)),
            jax.ShapeDtypeStruct((n, r, c), BF16))


def _ffn(h, gain, weights, layer, *, rows, batch_major_in=False, batch_major_out=False,
         final_gain=None, cast=(), f32_weights=False):
    w_gate, w_up, up_chunk, w_out = weights
    m = h.shape[1] * BATCH if batch_major_in else h.shape[0]
    steps = rows // BATCH
    n_tiles = m // rows
    nf = D_FF // FFN_COLS
    final_norm = final_gain is not None
    row_spec = pl.BlockSpec((rows, D_MODEL), lambda i, j: (i, 0))
    hbm_spec = pl.BlockSpec(memory_space=pl.ANY)
    in_specs = [
        hbm_spec if batch_major_in else row_spec,
        pl.BlockSpec((1, D_MODEL), lambda i, j: (0, 0)),
        pl.BlockSpec((None, D_MODEL, FFN_COLS), lambda i, j: (layer, 0, j)),
        pl.BlockSpec((None, D_MODEL, FFN_COLS), lambda i, j: (layer, 0, up_chunk + j)),
        pl.BlockSpec((None, FFN_COLS, D_MODEL), lambda i, j: (layer, j, 0)),
    ]
    args = [h, gain, w_gate, w_up, w_out]
    if final_norm:
        in_specs.append(pl.BlockSpec((1, D_MODEL), lambda i, j: (0, 0)))
        args.append(final_gain)
    cast_specs = [_cast_slab_specs(w, first, n_tiles, nf) for w, first in cast]
    in_specs += [c[0] for c in cast_specs]
    args += [w for w, _ in cast]
    out_shape = (BATCH, m // BATCH, D_MODEL) if batch_major_out else (m, D_MODEL)
    scratch = [pltpu.VMEM((rows, D_MODEL), BF16)]
    tile_buffers = [pltpu.VMEM((2, steps, BATCH, D_MODEL), F32), pltpu.SemaphoreType.DMA((2,))]
    if batch_major_in:
        scratch += tile_buffers
    if batch_major_out:
        scratch += tile_buffers
    manual = batch_major_in or batch_major_out
    copy_specs, copy_shapes = [], []
    if f32_weights:
        assert n_tiles == 1
        copy_specs = [pl.BlockSpec((1, D_MODEL, FFN_COLS), lambda i, j: (0, 0, j)),
                      pl.BlockSpec((1, D_MODEL, FFN_COLS), lambda i, j: (0, 0, j)),
                      pl.BlockSpec((1, FFN_COLS, D_MODEL), lambda i, j: (0, j, 0))]
        copy_shapes = [jax.ShapeDtypeStruct((1, D_MODEL, D_FF), BF16),
                       jax.ShapeDtypeStruct((1, D_MODEL, D_FF), BF16),
                       jax.ShapeDtypeStruct((1, D_FF, D_MODEL), BF16)]
    outs = pl.pallas_call(
        functools.partial(_ffn_kernel, rows=rows, n_tiles=n_tiles, batch_major_in=batch_major_in,
                          batch_major_out=batch_major_out, final_norm=final_norm,
                          n_cast=len(cast), f32_weights=f32_weights),
        grid=(n_tiles, nf),
        in_specs=in_specs,
        out_specs=([hbm_spec if batch_major_out else row_spec] + [c[1] for c in cast_specs]
                   + copy_specs),
        out_shape=([jax.ShapeDtypeStruct(out_shape, F32)] + [c[2] for c in cast_specs]
                   + copy_shapes),
        scratch_shapes=scratch,
        compiler_params=pltpu.CompilerParams(
            dimension_semantics=("arbitrary" if manual else "parallel", "arbitrary"),
            vmem_limit_bytes=V7X_VMEM_LIMIT_BYTES),
        name="ffn" + ("_from_bt" if batch_major_in else "") + ("_to_bt" if batch_major_out else ""),
    )(*args)
    return tuple(outs) if len(outs) > 1 else outs[0]


def _ffn_wide_kernel(x_ref, gain_ref, wgate_ref, wup_ref, wout_ref, *rest, rows, n_tiles, n_chunks,
                     layer, up_chunk, batch_major_out, final_norm):
    rest = list(rest)
    fgain_ref = rest.pop(0) if final_norm else None
    o_ref, n_ref, xbuf, x_sem, wg_buf, wu_buf, wo_buf, w_sem = (rest.pop(0) for _ in range(8))
    acc, out_sem = (rest.pop(0), rest.pop(0)) if batch_major_out else (None, None)
    assert not rest
    i = pl.program_id(0)
    j = pl.program_id(1)
    steps = rows // BATCH
    n_steps = (n_chunks + 1) // 2
    last_is_narrow = n_chunks % 2 == 1
    wide_cols = 2 * FFN_COLS
    slot = lax.rem(i * n_steps + j, 2)
    acc_slot = lax.rem(i, 2)

    def x_fetch(tile):
        return pltpu.make_async_copy(x_ref.at[pl.ds(tile * steps, steps)], xbuf, x_sem.at[0])

    def weight_fetches(step, into, cols):
        c0 = step * wide_cols
        if not isinstance(c0, int):
            c0 = pl.multiple_of(c0, wide_cols)
        return [
            pltpu.make_async_copy(wgate_ref.at[layer, :, pl.ds(c0, cols)],
                                  wg_buf.at[into, :, pl.ds(0, cols)], w_sem.at[into]),
            pltpu.make_async_copy(wup_ref.at[layer, :, pl.ds(up_chunk * FFN_COLS + c0, cols)],
                                  wu_buf.at[into, :, pl.ds(0, cols)], w_sem.at[into]),
            pltpu.make_async_copy(wout_ref.at[layer, pl.ds(c0, cols), :],
                                  wo_buf.at[into, pl.ds(0, cols), :], w_sem.at[into]),
        ]

    def writebacks(tile, outof):
        return [pltpu.make_async_copy(acc.at[outof, :, b, :],
                                      o_ref.at[b, pl.ds(tile * steps, steps), :], out_sem.at[outof])
                for b in range(BATCH)]

    def cols_of(step):
        return FFN_COLS if (last_is_narrow and step == n_steps - 1) else wide_cols

    @pl.when(j == 0)
    def _():
        @pl.when(i == 0)
        def _():
            x_fetch(0).start()
            for c in weight_fetches(0, 0, cols_of(0)):
                c.start()

        x_fetch(i).wait()
        if batch_major_out:
            @pl.when(i >= 2)
            def _():
                for c in writebacks(i - 2, acc_slot):
                    c.wait()

        def piece(s, carry):
            t0 = pl.multiple_of(s * NORM_STEPS, NORM_STEPS)
            r0 = pl.multiple_of(s * NORM_STEPS * BATCH, NORM_STEPS * BATCH)
            x = xbuf[pl.ds(t0, NORM_STEPS), :, :]
            if batch_major_out:
                acc[acc_slot, pl.ds(t0, NORM_STEPS), :, :] = x
            x = x.reshape(NORM_STEPS * BATCH, D_MODEL)
            if not batch_major_out:
                o_ref[pl.ds(r0, NORM_STEPS * BATCH), :] = x
            n_ref[pl.ds(r0, NORM_STEPS * BATCH), :] = _rms_norm(x, gain_ref[...]).astype(BF16)
            return carry

        lax.fori_loop(0, steps // NORM_STEPS, piece, 0)

        @pl.when(i + 1 < n_tiles)
        def _():
            x_fetch(i + 1).start()

    def start_fetches(step, cols):
        for c in weight_fetches(step, 1 - slot, cols):
            c.start()

    def step_body(cols):
        for c in weight_fetches(j, slot, cols):
            c.wait()
        nxt = j + 1
        pl.when(nxt < n_steps - 1)(functools.partial(start_fetches, nxt, wide_cols))
        if n_steps > 1:
            pl.when(nxt == n_steps - 1)(functools.partial(start_fetches, nxt, cols_of(n_steps - 1)))
        pl.when((nxt == n_steps) & (i + 1 < n_tiles))(functools.partial(start_fetches, 0, cols_of(0)))

        n = n_ref[...]
        for c0 in range(0, cols, FFN_COLS):
            chunk = slice(c0, c0 + FFN_COLS)
            g = jnp.dot(n, wg_buf[slot, :, chunk], preferred_element_type=F32)
            u = jnp.dot(n, wu_buf[slot, :, chunk], preferred_element_type=F32)
            act = (g * (0.25 * jnp.tanh(0.5 * g) + 0.25)) * u
            update = jnp.dot(act.astype(BF16), wo_buf[slot, chunk, :], preferred_element_type=F32)
            if batch_major_out:
                acc[acc_slot] += update.reshape(steps, BATCH, D_MODEL)
            else:
                o_ref[...] += update

    if last_is_narrow:
        pl.when(j < n_steps - 1)(functools.partial(step_body, wide_cols))
        pl.when(j == n_steps - 1)(functools.partial(step_body, FFN_COLS))
    else:
        step_body(wide_cols)

    if batch_major_out:
        @pl.when(j == n_steps - 1)
        def _():
            if final_norm:
                acc[acc_slot] = _rms_norm(acc[acc_slot], fgain_ref[...])
            for c in writebacks(i, acc_slot):
                c.start()

            @pl.when(i == n_tiles - 1)
            def _():
                if n_tiles >= 2:
                    for c in writebacks(i - 1, 1 - acc_slot):
                        c.wait()
                for c in writebacks(i, acc_slot):
                    c.wait()
    elif final_norm:
        @pl.when(j == n_steps - 1)
        def _():
            o_ref[...] = _rms_norm(o_ref[...], fgain_ref[...])


def _ffn_wide(h, gain, weights, layer, *, rows, batch_major_out=False, final_gain=None):
    w_gate, w_up, up_chunk, w_out = weights
    m = h.shape[0]
    steps = rows // BATCH
    n_tiles = m // rows
    nf = D_FF // FFN_COLS
    final_norm = final_gain is not None
    row_spec = pl.BlockSpec((rows, D_MODEL), lambda i, j: (i, 0))
    hbm_spec = pl.BlockSpec(memory_space=pl.ANY)
    in_specs = [hbm_spec, pl.BlockSpec((1, D_MODEL), lambda i, j: (0, 0)), hbm_spec, hbm_spec, hbm_spec]
    args = [h.reshape(m // BATCH, BATCH, D_MODEL), gain, w_gate, w_up, w_out]
    if final_norm:
        in_specs.append(pl.BlockSpec((1, D_MODEL), lambda i, j: (0, 0)))
        args.append(final_gain)
    out_shape = (BATCH, m // BATCH, D_MODEL) if batch_major_out else (m, D_MODEL)
    scratch = [
        pltpu.VMEM((rows, D_MODEL), BF16),
        pltpu.VMEM((steps, BATCH, D_MODEL), F32), pltpu.SemaphoreType.DMA((1,)),
        pltpu.VMEM((2, D_MODEL, 2 * FFN_COLS), BF16),
        pltpu.VMEM((2, D_MODEL, 2 * FFN_COLS), BF16),
        pltpu.VMEM((2, 2 * FFN_COLS, D_MODEL), BF16),
        pltpu.SemaphoreType.DMA((2,)),
    ]
    if batch_major_out:
        scratch += [pltpu.VMEM((2, steps, BATCH, D_MODEL), F32), pltpu.SemaphoreType.DMA((2,))]
    return pl.pallas_call(
        functools.partial(_ffn_wide_kernel, rows=rows, n_tiles=n_tiles, n_chunks=nf, layer=layer,
                          up_chunk=up_chunk, batch_major_out=batch_major_out, final_norm=final_norm),
        grid=(n_tiles, (nf + 1) // 2),
        in_specs=in_specs,
        out_specs=hbm_spec if batch_major_out else row_spec,
        out_shape=jax.ShapeDtypeStruct(out_shape, F32),
        scratch_shapes=scratch,
        compiler_params=pltpu.CompilerParams(
            dimension_semantics=("arbitrary", "arbitrary"),
            vmem_limit_bytes=V7X_VMEM_LIMIT_BYTES),
        name="ffn_wide" + ("_to_bt" if batch_major_out else ""),
    )(*args)


def _mix_kernel(x_ref, xprev_ref, gain_ref, win_ref, convw_ref, convb_ref, wgate_ref, ba_ref,
                bx_ref, aparam_ref, poolw_ref, poolb_ref, pools_ref, wout_ref,
                h0_ref, ctail0_ref, ptail0_ref,
                o_ref, hstate_ref, ctail_ref, ptail_ref,
                zx_ext, zp_ext, mix_s, *, rows, t_offset):
    s = pl.program_id(0)
    n_chunks = pl.num_programs(0) - 1
    chunk = jnp.minimum(s, n_chunks - 1)
    steps = rows // BATCH

    @pl.when(s == 0)
    def _():
        hstate_ref[...] = h0_ref[...]
        zx_ext[0:CONV_TAIL, :] = ctail0_ref[...]
        zp_ext[0:POOL_TAIL, :] = ptail0_ref[...]
        mix_s[...] = jnp.zeros_like(mix_s)

    n = _rms_norm(x_ref[...], gain_ref[...]).astype(BF16)
    z = jnp.dot(n, win_ref[...], preferred_element_type=F32)
    zx = z[:, :D_LRU]
    zg = z[:, D_LRU:2 * D_LRU]
    zp = z[:, 2 * D_LRU:]

    zx_ext[CONV_TAIL:CONV_TAIL + rows, :] = zx
    xc = convb_ref[...] + convw_ref[CONV_WIDTH - 1:CONV_WIDTH, :] * zx
    for k in range(CONV_WIDTH - 1):
        xc = xc + convw_ref[k:k + 1, :] * zx_ext[k * BATCH:k * BATCH + rows, :]
    ctail = zx_ext[rows:rows + CONV_TAIL, :]
    zx_ext[0:CONV_TAIL, :] = ctail

    ap = aparam_ref[...]
    neg_c_softplus = -LRU_C * (jnp.maximum(-ap, 0.0) + jnp.log1p(jnp.exp(-jnp.abs(ap))))
    xcb = xc.astype(BF16)
    head_cols = [slice(hd * LRU_HEAD_DIM, (hd + 1) * LRU_HEAD_DIM) for hd in range(LRU_HEADS)]
    gates = [jnp.dot(xcb[:, cols], wgate_ref[hd], preferred_element_type=F32)
             for hd, cols in enumerate(head_cols)]

    zp_ext[POOL_TAIL:POOL_TAIL + rows, :] = zp
    pooled = []
    for gidx, win in enumerate(POOL_WINDOWS):
        cols = slice(gidx * POOL_GROUP_DIM, (gidx + 1) * POOL_GROUP_DIM)
        ext = zp_ext[:, cols]
        acc = ext
        span = 1
        while span < win:
            acc = acc[span * BATCH:, :] + acc[:acc.shape[0] - span * BATCH, :]
            span *= 2
        acc = acc[acc.shape[0] - rows:, :]
        u = ext[POOL_TAIL:, :]
        if t_offset + 1 >= win:
            d = acc * (1.0 / win) - u
        else:
            row = lax.broadcasted_iota(jnp.int32, (rows, POOL_GROUP_DIM), 0)
            t_abs = lax.shift_right_logical(row, 3) + (chunk * steps + t_offset)
            d = acc / jnp.minimum(t_abs + 1, win).astype(F32) - u
        pooled.append(jnp.dot(d.astype(BF16), poolw_ref[gidx], preferred_element_type=F32))
    ptail = zp_ext[rows:rows + POOL_TAIL, :]
    zp_ext[0:POOL_TAIL, :] = ptail

    o_ref[...] = xprev_ref[...] + jnp.dot(mix_s[...], wout_ref[...], preferred_element_type=F32)

    h_prev = hstate_ref[...]
    h_new = []
    mixed = []
    for hd, cols in enumerate(head_cols):
        h = h_prev[:, cols]
        pieces = []
        for r0 in range(0, rows, SCAN_STEPS * BATCH):
            blk = slice(r0, r0 + SCAN_STEPS * BATCH)
            r = _sigmoid(gates[hd][blk, :LRU_HEAD_DIM] + ba_ref[:, cols])
            ig = _sigmoid(gates[hd][blk, LRU_HEAD_DIM:] + bx_ref[:, cols])
            t = jnp.tanh(0.5 * (r * neg_c_softplus[:, cols]))
            q = 1.0 / (1.0 - t)
            a = (1.0 + t) * q
            b = (2.0 * q * jnp.sqrt(-t)) * ig * xc[blk, cols]
            ys = []
            for st in range(SCAN_STEPS):
                rs = slice(st * BATCH, (st + 1) * BATCH)
                h = a[rs, :] * h + b[rs, :]
                ys.append(h)
            pieces.append((jnp.concatenate(ys, axis=0) * _gelu_tanh(zg[blk, cols])).astype(BF16))
        h_new.append(h)
        mixed.append(jnp.concatenate(pieces, axis=0))
    for gidx in range(len(POOL_WINDOWS)):
        cols = slice(gidx * POOL_GROUP_DIM, (gidx + 1) * POOL_GROUP_DIM)
        mixed.append(((pooled[gidx] + poolb_ref[:, cols]) * pools_ref[:, cols]).astype(BF16))
    mix_s[...] = jnp.concatenate(mixed, axis=1)

    @pl.when(s < n_chunks)
    def _():
        hstate_ref[...] = jnp.concatenate(h_new, axis=1)
        ctail_ref[...] = ctail
        ptail_ref[...] = ptail


def _mix(h, p, layer, state, *, rows, t_offset):
    m = h.shape[0]
    n_chunks = m // rows
    const = lambda s: (0, 0)

    def resident(shape):
        return pl.BlockSpec((None,) + shape, lambda s: (layer,) + (0,) * len(shape),
                            pipeline_mode=pl.Buffered(1))

    vec = lambda nl: pl.BlockSpec((1, nl), const)
    cur_spec = pl.BlockSpec((rows, D_MODEL), lambda s: (jnp.minimum(s, n_chunks - 1), 0))
    prev_spec = pl.BlockSpec((rows, D_MODEL), lambda s: (jnp.maximum(s - 1, 0), 0))
    in_specs = [
        cur_spec,
        prev_spec,
        vec(D_MODEL),
        resident((D_MODEL, D_IN)),
        pl.BlockSpec((CONV_WIDTH, D_LRU), const),
        vec(D_LRU),
        resident((LRU_HEADS, LRU_HEAD_DIM, 2 * LRU_HEAD_DIM)),
        vec(D_LRU), vec(D_LRU), vec(D_LRU),
        resident((len(POOL_WINDOWS), POOL_GROUP_DIM, POOL_GROUP_DIM)),
        vec(D_POOL), vec(D_POOL),
        resident((D_MIX, D_MODEL)),
        pl.BlockSpec((BATCH, D_LRU), const),
        pl.BlockSpec((CONV_TAIL, D_LRU), const),
        pl.BlockSpec((POOL_TAIL, D_POOL), const),
    ]
    out_specs = [
        prev_spec,
        pl.BlockSpec((BATCH, D_LRU), const),
        pl.BlockSpec((CONV_TAIL, D_LRU), const),
        pl.BlockSpec((POOL_TAIL, D_POOL), const),
    ]
    out_shape = [
        jax.ShapeDtypeStruct((m, D_MODEL), F32),
        jax.ShapeDtypeStruct((BATCH, D_LRU), F32),
        jax.ShapeDtypeStruct((CONV_TAIL, D_LRU), F32),
        jax.ShapeDtypeStruct((POOL_TAIL, D_POOL), F32),
    ]
    scratch = [
        pltpu.VMEM((rows + CONV_TAIL, D_LRU), F32),
        pltpu.VMEM((rows + POOL_TAIL, D_POOL), F32),
        pltpu.VMEM((rows, D_MIX), BF16),
    ]
    out, hstate, ctail, ptail = pl.pallas_call(
        functools.partial(_mix_kernel, rows=rows, t_offset=t_offset),
        grid=(n_chunks + 1,),
        in_specs=in_specs,
        out_specs=out_specs,
        out_shape=out_shape,
        scratch_shapes=scratch,
        compiler_params=pltpu.CompilerParams(
            dimension_semantics=("arbitrary",),
            vmem_limit_bytes=V7X_VMEM_LIMIT_BYTES),
        name="mix",
    )(h, h, p["gain"], p["w_in"], p["conv_w"], p["conv_b"], p["w_gate"], p["ba"], p["bx"],
      p["a_param"], p["pool_w"], p["pool_b"], p["pool_scale"], p["w_out"], *state)
    return out, (hstate, ctail, ptail)


def kernel(x, meta_tokens, ffn1_norm, ffn1_w_in, ffn1_w_out, mix_norm, w_in, conv_w, conv_b,
           lru_wa, lru_ba, lru_wx, lru_bx, lru_a_param, pool_w, pool_b, pool_scale, w_out,
           ffn2_norm, ffn2_w_in, ffn2_w_out, final_norm):
    b, t, d = x.shape
    assert (b, d) == (BATCH, D_MODEL) and (t * b) % FFN_ROWS == 0 and (t * b) % MIX_ROWS == 0

    hm = jnp.broadcast_to(meta_tokens.astype(x.dtype)[:, None, :], (N_META, b, d)).reshape(N_META * b, d)
    meta_rows = N_META * b
    hx = x

    row = lambda v: v.reshape(1, -1)
    zero_state = (jnp.zeros((BATCH, D_LRU), F32), jnp.zeros((CONV_TAIL, D_LRU), F32),
                  jnp.zeros((POOL_TAIL, D_POOL), F32))
    nf = D_FF // FFN_COLS
    mix_w = dict(w_in=w_in.astype(BF16),
                 w_gate=jnp.concatenate([lru_wa, lru_wx], axis=-1).astype(BF16),
                 pool_w=pool_w.astype(BF16), w_out=w_out.astype(BF16))
    hm, f1_gate, f1_up, f1_out = _ffn(hm, row(ffn1_norm[0]), (ffn1_w_in, ffn1_w_in, nf, ffn1_w_out),
                                      0, rows=meta_rows, f32_weights=True)
    f1_first = (f1_gate, f1_up, 0, f1_out)
    hx, f1_in_rest, f1_out_rest, f2_in, f2_out = _ffn(
        hx, row(ffn1_norm[0]), f1_first, 0, rows=FFN_ROWS, batch_major_in=True,
        cast=((ffn1_w_in, 1), (ffn1_w_out, 1), (ffn2_w_in, 0), (ffn2_w_out, 0)))
    f1_rest = (f1_in_rest, f1_in_rest, nf, f1_out_rest)
    f2_w = (f2_in, f2_in, nf, f2_out)

    for l in range(DEPTH):
        last = l == DEPTH - 1
        mp = dict(mix_w, gain=row(mix_norm[l]), conv_w=conv_w[l], conv_b=row(conv_b[l]),
                  ba=row(lru_ba[l]), bx=row(lru_bx[l]), a_param=row(lru_a_param[l]),
                  pool_b=row(pool_b[l]), pool_scale=row(pool_scale[l]))

        if l > 0:
            hm = _ffn(hm, row(ffn1_norm[l]), f1_rest, l - 1, rows=meta_rows)
            hx = _ffn_wide(hx, row(ffn1_norm[l]), f1_rest, l - 1, rows=FFN_ROWS)
        hm, meta_state = _mix(hm, mp, l, zero_state, rows=meta_rows, t_offset=0)
        hx, _ = _mix(hx, mp, l, meta_state, rows=MIX_ROWS, t_offset=N_META)
        if not last:
            hm = _ffn(hm, row(ffn2_norm[l]), f2_w, l, rows=meta_rows)
        hx = _ffn_wide(hx, row(ffn2_norm[l]), f2_w, l, rows=FFN_ROWS, batch_major_out=last,
                       final_gain=row(final_norm) if last else None)

    return hx
```

```python
import functools

import jax
import jax.numpy as jnp
from jax import lax
from jax.experimental import pallas as pl
from jax.experimental.pallas import tpu as pltpu

D_MODEL = 2048
BATCH = 8
DEPTH = 2
N_META = 16
D_LRU = D_MODEL // 2
LRU_HEADS = 8
LRU_HEAD_DIM = D_LRU // LRU_HEADS
CONV_WIDTH = 4
LRU_C = 8.0
D_POOL = D_MODEL // 2
POOL_WINDOWS = (2, 4, 8, 16)
POOL_GROUP_DIM = D_POOL // len(POOL_WINDOWS)
D_MIX = D_LRU + D_POOL
D_IN = 2 * D_LRU + D_POOL
D_FF = ((8 * D_MODEL // 3 + 255) // 256) * 256
RMS_EPS = 1e-6

SUBLANES = 8
LANES = 128
BF16_TILE_ROWS = 16
CONV_TAIL = (CONV_WIDTH - 1) * BATCH
POOL_TAIL = max(POOL_WINDOWS) * BATCH
V7X_VMEM_LIMIT_BYTES = 60 * 1024 * 1024

FFN_ROWS = 1024
FFN_COLS = 512
MIX_ROWS = 512
SCAN_STEPS = 16
NORM_STEPS = 64

F32 = jnp.float32
BF16 = jnp.bfloat16

assert BATCH == SUBLANES
assert D_FF % FFN_COLS == 0


def _rms_norm(x, gain):
    return x * lax.rsqrt(jnp.mean(x * x, axis=-1, keepdims=True) + RMS_EPS) * gain


def _sigmoid(x):
    return 0.5 * jnp.tanh(0.5 * x) + 0.5


def _gelu_tanh(x):
    return 0.5 * x * (1.0 + jnp.tanh(0.7978845608028654 * (x + 0.044715 * (x * x * x))))


def _ffn_kernel(x_ref, gain_ref, wg_ref, wu_ref, wo_ref, *rest, rows, n_tiles, batch_major_in,
                batch_major_out, final_norm, n_cast, f32_weights):
    rest = list(rest)
    fgain_ref = rest.pop(0) if final_norm else None
    cast_src = [rest.pop(0) for _ in range(n_cast)]
    o_ref = rest.pop(0)
    cast_dst = [rest.pop(0) for _ in range(n_cast)]
    bf16_copies = [rest.pop(0) for _ in range(3)] if f32_weights else None
    n_ref = rest.pop(0)
    xbuf, in_sem = (rest.pop(0), rest.pop(0)) if batch_major_in else (None, None)
    acc, out_sem = (rest.pop(0), rest.pop(0)) if batch_major_out else (None, None)
    assert not rest
    i = pl.program_id(0)
    j = pl.program_id(1)
    last_j = pl.num_programs(1) - 1
    slot = lax.rem(i, 2)
    steps = rows // BATCH

    def fetches(tile, into):
        return [pltpu.make_async_copy(x_ref.at[b, pl.ds(tile * steps, steps), :],
                                      xbuf.at[into, :, b, :], in_sem.at[into])
                for b in range(BATCH)]

    def writebacks(tile, outof):
        return [pltpu.make_async_copy(acc.at[outof, :, b, :],
                                      o_ref.at[b, pl.ds(tile * steps, steps), :], out_sem.at[outof])
                for b in range(BATCH)]

    for src, dst in zip(cast_src, cast_dst):
        dst[...] = src[...].astype(BF16)

    @pl.when(j == 0)
    def _():
        if batch_major_in:
            @pl.when(i == 0)
            def _():
                for c in fetches(0, 0):
                    c.start()

            for c in fetches(i, slot):
                c.wait()

            @pl.when(i + 1 < n_tiles)
            def _():
                for c in fetches(i + 1, 1 - slot):
                    c.start()

        if batch_major_out:
            @pl.when(i >= 2)
            def _():
                for c in writebacks(i - 2, slot):
                    c.wait()

        pst = min(NORM_STEPS, steps)

        def piece(s, carry):
            t0 = pl.multiple_of(s * pst, pst)
            r0 = pl.multiple_of(s * pst * BATCH, pst * BATCH)
            if batch_major_in:
                x = xbuf[slot, pl.ds(t0, pst), :, :].reshape(pst * BATCH, D_MODEL)
            else:
                x = x_ref[pl.ds(r0, pst * BATCH), :]
            if batch_major_out:
                acc[slot, pl.ds(t0, pst), :, :] = x.reshape(pst, BATCH, D_MODEL)
            else:
                o_ref[pl.ds(r0, pst * BATCH), :] = x
            n_ref[pl.ds(r0, pst * BATCH), :] = _rms_norm(x, gain_ref[...]).astype(BF16)
            return carry

        lax.fori_loop(0, steps // pst, piece, 0)

    wg, wu, wo = wg_ref[...], wu_ref[...], wo_ref[...]
    if f32_weights:
        wg, wu, wo = wg.astype(BF16), wu.astype(BF16), wo.astype(BF16)
        for dst, w in zip(bf16_copies, (wg, wu, wo)):
            dst[0] = w

    n = n_ref[...]
    g = jnp.dot(n, wg, preferred_element_type=F32)
    u = jnp.dot(n, wu, preferred_element_type=F32)
    act = (g * (0.25 * jnp.tanh(0.5 * g) + 0.25)) * u
    update = jnp.dot(act.astype(BF16), wo, preferred_element_type=F32)
    if batch_major_out:
        acc[slot] += update.reshape(steps, BATCH, D_MODEL)
    else:
        o_ref[...] += update

    if batch_major_out:
        @pl.when(j == last_j)
        def _():
            if final_norm:
                acc[slot] = _rms_norm(acc[slot], fgain_ref[...])
            for c in writebacks(i, slot):
                c.start()

            @pl.when(i == n_tiles - 1)
            def _():
                if n_tiles >= 2:
                    for c in writebacks(i - 1, 1 - slot):
                        c.wait()
                for c in writebacks(i, slot):
                    c.wait()
    elif final_norm:
        @pl.when(j == last_j)
        def _():
            o_ref[...] = _rms_norm(o_ref[...], fgain_ref[...])


def _cast_slab_specs(w, first_layer, n_tiles, n_chunks):
    layers, r, c = w.shape
    n = layers - first_layer
    assert first_layer % n == 0
    if r % (n_tiles * BF16_TILE_ROWS) == 0 and c % (n_chunks * LANES) == 0:
        block = (n, r // n_tiles, c // n_chunks)
        pos = lambda i, j: (i, j)
    else:
        assert r % (n_tiles * n_chunks * BF16_TILE_ROWS) == 0 and c % LANES == 0
        block = (n, r // (n_tiles * n_chunks), c)
        pos = lambda i, j: (i * n_chunks + j, 0)
    return (pl.BlockSpec(block, lambda i, j: (first_layer // n,) + pos(i, j)),
            pl.BlockSpec(block, lambda i, j: (0,) + pos(i, j)),
            jax.ShapeDtypeStruct((n, r, c), BF16))


def _ffn(h, gain, weights, layer, *, rows, batch_major_in=False, batch_major_out=False,
         final_gain=None, cast=(), f32_weights=False):
    w_gate, w_up, up_chunk, w_out = weights
    m = h.shape[1] * BATCH if batch_major_in else h.shape[0]
    steps = rows // BATCH
    n_tiles = m // rows
    nf = D_FF // FFN_COLS
    final_norm = final_gain is not None
    row_spec = pl.BlockSpec((rows, D_MODEL), lambda i, j: (i, 0))
    hbm_spec = pl.BlockSpec(memory_space=pl.ANY)
    in_specs = [
        hbm_spec if batch_major_in else row_spec,
        pl.BlockSpec((1, D_MODEL), lambda i, j: (0, 0)),
        pl.BlockSpec((None, D_MODEL, FFN_COLS), lambda i, j: (layer, 0, j)),
        pl.BlockSpec((None, D_MODEL, FFN_COLS), lambda i, j: (layer, 0, up_chunk + j)),
        pl.BlockSpec((None, FFN_COLS, D_MODEL), lambda i, j: (layer, j, 0)),
    ]
    args = [h, gain, w_gate, w_up, w_out]
    if final_norm:
        in_specs.append(pl.BlockSpec((1, D_MODEL), lambda i, j: (0, 0)))
        args.append(final_gain)
    cast_specs = [_cast_slab_specs(w, first, n_tiles, nf) for w, first in cast]
    in_specs += [c[0] for c in cast_specs]
    args += [w for w, _ in cast]
    out_shape = (BATCH, m // BATCH, D_MODEL) if batch_major_out else (m, D_MODEL)
    scratch = [pltpu.VMEM((rows, D_MODEL), BF16)]
    tile_buffers = [pltpu.VMEM((2, steps, BATCH, D_MODEL), F32), pltpu.SemaphoreType.DMA((2,))]
    if batch_major_in:
        scratch += tile_buffers
    if batch_major_out:
        scratch += tile_buffers
    manual = batch_major_in or batch_major_out
    copy_specs, copy_shapes = [], []
    if f32_weights:
        assert n_tiles == 1
        copy_specs = [pl.BlockSpec((1, D_MODEL, FFN_COLS), lambda i, j: (0, 0, j)),
                      pl.BlockSpec((1, D_MODEL, FFN_COLS), lambda i, j: (0, 0, j)),
                      pl.BlockSpec((1, FFN_COLS, D_MODEL), lambda i, j: (0, j, 0))]
        copy_shapes = [jax.ShapeDtypeStruct((1, D_MODEL, D_FF), BF16),
                       jax.ShapeDtypeStruct((1, D_MODEL, D_FF), BF16),
                       jax.ShapeDtypeStruct((1, D_FF, D_MODEL), BF16)]
    outs = pl.pallas_call(
        functools.partial(_ffn_kernel, rows=rows, n_tiles=n_tiles, batch_major_in=batch_major_in,
                          batch_major_out=batch_major_out, final_norm=final_norm,
                          n_cast=len(cast), f32_weights=f32_weights),
        grid=(n_tiles, nf),
        in_specs=in_specs,
        out_specs=([hbm_spec if batch_major_out else row_spec] + [c[1] for c in cast_specs]
                   + copy_specs),
        out_shape=([jax.ShapeDtypeStruct(out_shape, F32)] + [c[2] for c in cast_specs]
                   + copy_shapes),
        scratch_shapes=scratch,
        compiler_params=pltpu.CompilerParams(
            dimension_semantics=("arbitrary" if manual else "parallel", "arbitrary"),
            vmem_limit_bytes=V7X_VMEM_LIMIT_BYTES),
        name="ffn" + ("_from_bt" if batch_major_in else "") + ("_to_bt" if batch_major_out else ""),
    )(*args)
    return tuple(outs) if len(outs) > 1 else outs[0]


def _ffn_wide_kernel(x_ref, gain_ref, wgate_ref, wup_ref, wout_ref, *rest, rows, n_tiles, n_chunks,
                     layer, up_chunk, batch_major_out, final_norm):
    rest = list(rest)
    fgain_ref = rest.pop(0) if final_norm else None
    o_ref, n_ref, xbuf, x_sem, wg_buf, wu_buf, wo_buf, w_sem = (rest.pop(0) for _ in range(8))
    acc, out_sem = (rest.pop(0), rest.pop(0)) if batch_major_out else (None, None)
    assert not rest
    i = pl.program_id(0)
    j = pl.program_id(1)
    steps = rows // BATCH
    n_steps = (n_chunks + 1) // 2
    last_is_narrow = n_chunks % 2 == 1
    wide_cols = 2 * FFN_COLS
    slot = lax.rem(i * n_steps + j, 2)
    acc_slot = lax.rem(i, 2)

    def x_fetch(tile):
        return pltpu.make_async_copy(x_ref.at[pl.ds(tile * steps, steps)], xbuf, x_sem.at[0])

    def weight_fetches(step, into, cols):
        c0 = step * wide_cols
        if not isinstance(c0, int):
            c0 = pl.multiple_of(c0, wide_cols)
        return [
            pltpu.make_async_copy(wgate_ref.at[layer, :, pl.ds(c0, cols)],
                                  wg_buf.at[into, :, pl.ds(0, cols)], w_sem.at[into]),
            pltpu.make_async_copy(wup_ref.at[layer, :, pl.ds(up_chunk * FFN_COLS + c0, cols)],
                                  wu_buf.at[into, :, pl.ds(0, cols)], w_sem.at[into]),
            pltpu.make_async_copy(wout_ref.at[layer, pl.ds(c0, cols), :],
                                  wo_buf.at[into, pl.ds(0, cols), :], w_sem.at[into]),
        ]

    def writebacks(tile, outof):
        return [pltpu.make_async_copy(acc.at[outof, :, b, :],
                                      o_ref.at[b, pl.ds(tile * steps, steps), :], out_sem.at[outof])
                for b in range(BATCH)]

    def cols_of(step):
        return FFN_COLS if (last_is_narrow and step == n_steps - 1) else wide_cols

    @pl.when(j == 0)
    def _():
        @pl.when(i == 0)
        def _():
            x_fetch(0).start()
            for c in weight_fetches(0, 0, cols_of(0)):
                c.start()

        x_fetch(i).wait()
        if batch_major_out:
            @pl.when(i >= 2)
            def _():
                for c in writebacks(i - 2, acc_slot):
                    c.wait()

        def piece(s, carry):
            t0 = pl.multiple_of(s * NORM_STEPS, NORM_STEPS)
            r0 = pl.multiple_of(s * NORM_STEPS * BATCH, NORM_STEPS * BATCH)
            x = xbuf[pl.ds(t0, NORM_STEPS), :, :]
            if batch_major_out:
                acc[acc_slot, pl.ds(t0, NORM_STEPS), :, :] = x
            x = x.reshape(NORM_STEPS * BATCH, D_MODEL)
            if not batch_major_out:
                o_ref[pl.ds(r0, NORM_STEPS * BATCH), :] = x
            n_ref[pl.ds(r0, NORM_STEPS * BATCH), :] = _rms_norm(x, gain_ref[...]).astype(BF16)
            return carry

        lax.fori_loop(0, steps // NORM_STEPS, piece, 0)

        @pl.when(i + 1 < n_tiles)
        def _():
            x_fetch(i + 1).start()

    def start_fetches(step, cols):
        for c in weight_fetches(step, 1 - slot, cols):
            c.start()

    def step_body(cols):
        nxt = j + 1
        pl.when(nxt < n_steps - 1)(functools.partial(start_fetches, nxt, wide_cols))
        if n_steps > 1:
            pl.when(nxt == n_steps - 1)(functools.partial(start_fetches, nxt, cols_of(n_steps - 1)))
        pl.when((nxt == n_steps) & (i + 1 < n_tiles))(functools.partial(start_fetches, 0, cols_of(0)))
        for c in weight_fetches(j, slot, cols):
            c.wait()

        n = n_ref[...]
        for c0 in range(0, cols, FFN_COLS):
            chunk = slice(c0, c0 + FFN_COLS)
            g = jnp.dot(n, wg_buf[slot, :, chunk], preferred_element_type=F32)
            u = jnp.dot(n, wu_buf[slot, :, chunk], preferred_element_type=F32)
            act = (g * (0.25 * jnp.tanh(0.5 * g) + 0.25)) * u
            update = jnp.dot(act.astype(BF16), wo_buf[slot, chunk, :], preferred_element_type=F32)
            if batch_major_out:
                acc[acc_slot] += update.reshape(steps, BATCH, D_MODEL)
            else:
                o_ref[...] += update

    if last_is_narrow:
        pl.when(j < n_steps - 1)(functools.partial(step_body, wide_cols))
        pl.when(j == n_steps - 1)(functools.partial(step_body, FFN_COLS))
    else:
        step_body(wide_cols)

    if batch_major_out:
        @pl.when(j == n_steps - 1)
        def _():
            if final_norm:
                acc[acc_slot] = _rms_norm(acc[acc_slot], fgain_ref[...])
            for c in writebacks(i, acc_slot):
                c.start()

            @pl.when(i == n_tiles - 1)
            def _():
                if n_tiles >= 2:
                    for c in writebacks(i - 1, 1 - acc_slot):
                        c.wait()
                for c in writebacks(i, acc_slot):
                    c.wait()
    elif final_norm:
        @pl.when(j == n_steps - 1)
        def _():
            o_ref[...] = _rms_norm(o_ref[...], fgain_ref[...])


def _ffn_wide(h, gain, weights, layer, *, rows, batch_major_out=False, final_gain=None):
    w_gate, w_up, up_chunk, w_out = weights
    m = h.shape[0]
    steps = rows // BATCH
    n_tiles = m // rows
    nf = D_FF // FFN_COLS
    final_norm = final_gain is not None
    row_spec = pl.BlockSpec((rows, D_MODEL), lambda i, j: (i, 0))
    hbm_spec = pl.BlockSpec(memory_space=pl.ANY)
    in_specs = [hbm_spec, pl.BlockSpec((1, D_MODEL), lambda i, j: (0, 0)), hbm_spec, hbm_spec, hbm_spec]
    args = [h.reshape(m // BATCH, BATCH, D_MODEL), gain, w_gate, w_up, w_out]
    if final_norm:
        in_specs.append(pl.BlockSpec((1, D_MODEL), lambda i, j: (0, 0)))
        args.append(final_gain)
    out_shape = (BATCH, m // BATCH, D_MODEL) if batch_major_out else (m, D_MODEL)
    scratch = [
        pltpu.VMEM((rows, D_MODEL), BF16),
        pltpu.VMEM((steps, BATCH, D_MODEL), F32), pltpu.SemaphoreType.DMA((1,)),
        pltpu.VMEM((2, D_MODEL, 2 * FFN_COLS), BF16),
        pltpu.VMEM((2, D_MODEL, 2 * FFN_COLS), BF16),
        pltpu.VMEM((2, 2 * FFN_COLS, D_MODEL), BF16),
        pltpu.SemaphoreType.DMA((2,)),
    ]
    if batch_major_out:
        scratch += [pltpu.VMEM((2, steps, BATCH, D_MODEL), F32), pltpu.SemaphoreType.DMA((2,))]
    return pl.pallas_call(
        functools.partial(_ffn_wide_kernel, rows=rows, n_tiles=n_tiles, n_chunks=nf, layer=layer,
                          up_chunk=up_chunk, batch_major_out=batch_major_out, final_norm=final_norm),
        grid=(n_tiles, (nf + 1) // 2),
        in_specs=in_specs,
        out_specs=hbm_spec if batch_major_out else row_spec,
        out_shape=jax.ShapeDtypeStruct(out_shape, F32),
        scratch_shapes=scratch,
        compiler_params=pltpu.CompilerParams(
            dimension_semantics=("arbitrary", "arbitrary"),
            vmem_limit_bytes=V7X_VMEM_LIMIT_BYTES),
        name="ffn_wide" + ("_to_bt" if batch_major_out else ""),
    )(*args)


def _mix_kernel(x_ref, xprev_ref, gain_ref, win_ref, convw_ref, convb_ref, wgate_ref, ba_ref,
                bx_ref, aparam_ref, poolw_ref, poolb_ref, pools_ref, wout_ref,
                h0_ref, ctail0_ref, ptail0_ref,
                o_ref, hstate_ref, ctail_ref, ptail_ref,
                zx_ext, zp_ext, mix_s, *, rows, t_offset):
    s = pl.program_id(0)
    n_chunks = pl.num_programs(0) - 1
    chunk = jnp.minimum(s, n_chunks - 1)
    steps = rows // BATCH

    @pl.when(s == 0)
    def _():
        hstate_ref[...] = h0_ref[...]
        zx_ext[0:CONV_TAIL, :] = ctail0_ref[...]
        zp_ext[0:POOL_TAIL, :] = ptail0_ref[...]
        mix_s[...] = jnp.zeros_like(mix_s)

    n = _rms_norm(x_ref[...], gain_ref[...]).astype(BF16)
    z = jnp.dot(n, win_ref[...], preferred_element_type=F32)
    zx = z[:, :D_LRU]
    zg = z[:, D_LRU:2 * D_LRU]
    zp = z[:, 2 * D_LRU:]

    zx_ext[CONV_TAIL:CONV_TAIL + rows, :] = zx
    xc = convb_ref[...] + convw_ref[CONV_WIDTH - 1:CONV_WIDTH, :] * zx
    for k in range(CONV_WIDTH - 1):
        xc = xc + convw_ref[k:k + 1, :] * zx_ext[k * BATCH:k * BATCH + rows, :]
    ctail = zx_ext[rows:rows + CONV_TAIL, :]
    zx_ext[0:CONV_TAIL, :] = ctail

    ap = aparam_ref[...]
    neg_c_softplus = -LRU_C * (jnp.maximum(-ap, 0.0) + jnp.log1p(jnp.exp(-jnp.abs(ap))))
    xcb = xc.astype(BF16)
    head_cols = [slice(hd * LRU_HEAD_DIM, (hd + 1) * LRU_HEAD_DIM) for hd in range(LRU_HEADS)]
    gates = [jnp.dot(xcb[:, cols], wgate_ref[hd], preferred_element_type=F32)
             for hd, cols in enumerate(head_cols)]

    zp_ext[POOL_TAIL:POOL_TAIL + rows, :] = zp
    pooled = []
    for gidx, win in enumerate(POOL_WINDOWS):
        cols = slice(gidx * POOL_GROUP_DIM, (gidx + 1) * POOL_GROUP_DIM)
        ext = zp_ext[:, cols]
        acc = ext
        span = 1
        while span < win:
            acc = acc[span * BATCH:, :] + acc[:acc.shape[0] - span * BATCH, :]
            span *= 2
        acc = acc[acc.shape[0] - rows:, :]
        u = ext[POOL_TAIL:, :]
        if t_offset + 1 >= win:
            d = acc * (1.0 / win) - u
        else:
            row = lax.broadcasted_iota(jnp.int32, (rows, POOL_GROUP_DIM), 0)
            t_abs = lax.shift_right_logical(row, 3) + (chunk * steps + t_offset)
            d = acc / jnp.minimum(t_abs + 1, win).astype(F32) - u
        pooled.append(jnp.dot(d.astype(BF16), poolw_ref[gidx], preferred_element_type=F32))
    ptail = zp_ext[rows:rows + POOL_TAIL, :]
    zp_ext[0:POOL_TAIL, :] = ptail

    o_ref[...] = xprev_ref[...] + jnp.dot(mix_s[...], wout_ref[...], preferred_element_type=F32)

    h_prev = hstate_ref[...]
    h_new = []
    mixed = []
    for hd, cols in enumerate(head_cols):
        h = h_prev[:, cols]
        pieces = []
        for r0 in range(0, rows, SCAN_STEPS * BATCH):
            blk = slice(r0, r0 + SCAN_STEPS * BATCH)
            r = _sigmoid(gates[hd][blk, :LRU_HEAD_DIM] + ba_ref[:, cols])
            ig = _sigmoid(gates[hd][blk, LRU_HEAD_DIM:] + bx_ref[:, cols])
            t = jnp.tanh(0.5 * (r * neg_c_softplus[:, cols]))
            q = 1.0 / (1.0 - t)
            a = (1.0 + t) * q
            b = (2.0 * q * jnp.sqrt(-t)) * ig * xc[blk, cols]
            ys = []
            for st in range(SCAN_STEPS):
                rs = slice(st * BATCH, (st + 1) * BATCH)
                h = a[rs, :] * h + b[rs, :]
                ys.append(h)
            pieces.append((jnp.concatenate(ys, axis=0) * _gelu_tanh(zg[blk, cols])).astype(BF16))
        h_new.append(h)
        mixed.append(jnp.concatenate(pieces, axis=0))
    for gidx in range(len(POOL_WINDOWS)):
        cols = slice(gidx * POOL_GROUP_DIM, (gidx + 1) * POOL_GROUP_DIM)
        mixed.append(((pooled[gidx] + poolb_ref[:, cols]) * pools_ref[:, cols]).astype(BF16))
    mix_s[...] = jnp.concatenate(mixed, axis=1)

    @pl.when(s < n_chunks)
    def _():
        hstate_ref[...] = jnp.concatenate(h_new, axis=1)
        ctail_ref[...] = ctail
        ptail_ref[...] = ptail


def _mix(h, p, layer, state, *, rows, t_offset):
    m = h.shape[0]
    n_chunks = m // rows
    const = lambda s: (0, 0)

    def resident(shape):
        return pl.BlockSpec((None,) + shape, lambda s: (layer,) + (0,) * len(shape),
                            pipeline_mode=pl.Buffered(1))

    vec = lambda nl: pl.BlockSpec((1, nl), const)
    cur_spec = pl.BlockSpec((rows, D_MODEL), lambda s: (jnp.minimum(s, n_chunks - 1), 0))
    prev_spec = pl.BlockSpec((rows, D_MODEL), lambda s: (jnp.maximum(s - 1, 0), 0))
    in_specs = [
        cur_spec,
        prev_spec,
        vec(D_MODEL),
        resident((D_MODEL, D_IN)),
        pl.BlockSpec((CONV_WIDTH, D_LRU), const),
        vec(D_LRU),
        resident((LRU_HEADS, LRU_HEAD_DIM, 2 * LRU_HEAD_DIM)),
        vec(D_LRU), vec(D_LRU), vec(D_LRU),
        resident((len(POOL_WINDOWS), POOL_GROUP_DIM, POOL_GROUP_DIM)),
        vec(D_POOL), vec(D_POOL),
        resident((D_MIX, D_MODEL)),
        pl.BlockSpec((BATCH, D_LRU), const),
        pl.BlockSpec((CONV_TAIL, D_LRU), const),
        pl.BlockSpec((POOL_TAIL, D_POOL), const),
    ]
    out_specs = [
        prev_spec,
        pl.BlockSpec((BATCH, D_LRU), const),
        pl.BlockSpec((CONV_TAIL, D_LRU), const),
        pl.BlockSpec((POOL_TAIL, D_POOL), const),
    ]
    out_shape = [
        jax.ShapeDtypeStruct((m, D_MODEL), F32),
        jax.ShapeDtypeStruct((BATCH, D_LRU), F32),
        jax.ShapeDtypeStruct((CONV_TAIL, D_LRU), F32),
        jax.ShapeDtypeStruct((POOL_TAIL, D_POOL), F32),
    ]
    scratch = [
        pltpu.VMEM((rows + CONV_TAIL, D_LRU), F32),
        pltpu.VMEM((rows + POOL_TAIL, D_POOL), F32),
        pltpu.VMEM((rows, D_MIX), BF16),
    ]
    out, hstate, ctail, ptail = pl.pallas_call(
        functools.partial(_mix_kernel, rows=rows, t_offset=t_offset),
        grid=(n_chunks + 1,),
        in_specs=in_specs,
        out_specs=out_specs,
        out_shape=out_shape,
        scratch_shapes=scratch,
        compiler_params=pltpu.CompilerParams(
            dimension_semantics=("arbitrary",),
            vmem_limit_bytes=V7X_VMEM_LIMIT_BYTES),
        name="mix",
    )(h, h, p["gain"], p["w_in"], p["conv_w"], p["conv_b"], p["w_gate"], p["ba"], p["bx"],
      p["a_param"], p["pool_w"], p["pool_b"], p["pool_scale"], p["w_out"], *state)
    return out, (hstate, ctail, ptail)


def kernel(x, meta_tokens, ffn1_norm, ffn1_w_in, ffn1_w_out, mix_norm, w_in, conv_w, conv_b,
           lru_wa, lru_ba, lru_wx, lru_bx, lru_a_param, pool_w, pool_b, pool_scale, w_out,
           ffn2_norm, ffn2_w_in, ffn2_w_out, final_norm):
    b, t, d = x.shape
    assert (b, d) == (BATCH, D_MODEL) and (t * b) % FFN_ROWS == 0 and (t * b) % MIX_ROWS == 0

    hm = jnp.broadcast_to(meta_tokens.astype(x.dtype)[:, None, :], (N_META, b, d)).reshape(N_META * b, d)
    meta_rows = N_META * b
    hx = x

    row = lambda v: v.reshape(1, -1)
    zero_state = (jnp.zeros((BATCH, D_LRU), F32), jnp.zeros((CONV_TAIL, D_LRU), F32),
                  jnp.zeros((POOL_TAIL, D_POOL), F32))
    nf = D_FF // FFN_COLS
    mix_w = dict(w_in=w_in.astype(BF16),
                 w_gate=jnp.concatenate([lru_wa, lru_wx], axis=-1).astype(BF16),
                 pool_w=pool_w.astype(BF16), w_out=w_out.astype(BF16))
    hm, f1_gate, f1_up, f1_out = _ffn(hm, row(ffn1_norm[0]), (ffn1_w_in, ffn1_w_in, nf, ffn1_w_out),
                                      0, rows=meta_rows, f32_weights=True)
    f1_first = (f1_gate, f1_up, 0, f1_out)
    hx, f1_in_rest, f1_out_rest, f2_in, f2_out = _ffn(
        hx, row(ffn1_norm[0]), f1_first, 0, rows=FFN_ROWS, batch_major_in=True,
        cast=((ffn1_w_in, 1), (ffn1_w_out, 1), (ffn2_w_in, 0), (ffn2_w_out, 0)))
    f1_rest = (f1_in_rest, f1_in_rest, nf, f1_out_rest)
    f2_w = (f2_in, f2_in, nf, f2_out)

    for l in range(DEPTH):
        last = l == DEPTH - 1
        mp = dict(mix_w, gain=row(mix_norm[l]), conv_w=conv_w[l], conv_b=row(conv_b[l]),
                  ba=row(lru_ba[l]), bx=row(lru_bx[l]), a_param=row(lru_a_param[l]),
                  pool_b=row(pool_b[l]), pool_scale=row(pool_scale[l]))

        if l > 0:
            hm = _ffn(hm, row(ffn1_norm[l]), f1_rest, l - 1, rows=meta_rows)
            hx = _ffn_wide(hx, row(ffn1_norm[l]), f1_rest, l - 1, rows=FFN_ROWS)
        hm, meta_state = _mix(hm, mp, l, zero_state, rows=meta_rows, t_offset=0)
        hx, _ = _mix(hx, mp, l, meta_state, rows=MIX_ROWS, t_offset=N_META)
        if not last:
            hm = _ffn(hm, row(ffn2_norm[l]), f2_w, l, rows=meta_rows)
        hx = _ffn_wide(hx, row(ffn2_norm[l]), f2_w, l, rows=FFN_ROWS, batch_major_out=last,
                       final_gain=row(final_norm) if last else None)

    return hx
```

```python
import functools

import jax
import jax.numpy as jnp
from jax import lax
from jax.experimental import pallas as pl
from jax.experimental.pallas import tpu as pltpu

D_MODEL = 2048
BATCH = 8
DEPTH = 2
N_META = 16
D_LRU = D_MODEL // 2
LRU_HEADS = 8
LRU_HEAD_DIM = D_LRU // LRU_HEADS
CONV_WIDTH = 4
LRU_C = 8.0
D_POOL = D_MODEL // 2
POOL_WINDOWS = (2, 4, 8, 16)
POOL_GROUP_DIM = D_POOL // len(POOL_WINDOWS)
D_MIX = D_LRU + D_POOL
D_IN = 2 * D_LRU + D_POOL
D_FF = ((8 * D_MODEL // 3 + 255) // 256) * 256
RMS_EPS = 1e-6

SUBLANES = 8
LANES = 128
BF16_TILE_ROWS = 16
CONV_TAIL = (CONV_WIDTH - 1) * BATCH
POOL_TAIL = max(POOL_WINDOWS) * BATCH
V7X_VMEM_LIMIT_BYTES = 60 * 1024 * 1024

FFN_ROWS = 1024
FFN_COLS = 512
MIX_ROWS = 512
SCAN_STEPS = 16
NORM_STEPS = 64

F32 = jnp.float32
BF16 = jnp.bfloat16

assert BATCH == SUBLANES
assert D_FF % FFN_COLS == 0


def _rms_norm(x, gain):
    return x * lax.rsqrt(jnp.mean(x * x, axis=-1, keepdims=True) + RMS_EPS) * gain


def _sigmoid(x):
    return 0.5 * jnp.tanh(0.5 * x) + 0.5


def _gelu_tanh(x):
    return 0.5 * x * (1.0 + jnp.tanh(0.7978845608028654 * (x + 0.044715 * (x * x * x))))


def _ffn_kernel(x_ref, gain_ref, wg_ref, wu_ref, wo_ref, *rest, rows, n_tiles, batch_major_in,
                batch_major_out, final_norm, n_cast, f32_weights):
    rest = list(rest)
    fgain_ref = rest.pop(0) if final_norm else None
    cast_src = [rest.pop(0) for _ in range(n_cast)]
    o_ref = rest.pop(0)
    cast_dst = [rest.pop(0) for _ in range(n_cast)]
    bf16_copies = [rest.pop(0) for _ in range(3)] if f32_weights else None
    n_ref = rest.pop(0)
    xbuf, in_sem = (rest.pop(0), rest.pop(0)) if batch_major_in else (None, None)
    acc, out_sem = (rest.pop(0), rest.pop(0)) if batch_major_out else (None, None)
    assert not rest
    i = pl.program_id(0)
    j = pl.program_id(1)
    last_j = pl.num_programs(1) - 1
    slot = lax.rem(i, 2)
    steps = rows // BATCH

    def fetches(tile, into):
        return [pltpu.make_async_copy(x_ref.at[b, pl.ds(tile * steps, steps), :],
                                      xbuf.at[into, :, b, :], in_sem.at[into])
                for b in range(BATCH)]

    def writebacks(tile, outof):
        return [pltpu.make_async_copy(acc.at[outof, :, b, :],
                                      o_ref.at[b, pl.ds(tile * steps, steps), :], out_sem.at[outof])
                for b in range(BATCH)]

    for src, dst in zip(cast_src, cast_dst):
        dst[...] = src[...].astype(BF16)

    @pl.when(j == 0)
    def _():
        if batch_major_in:
            @pl.when(i == 0)
            def _():
                for c in fetches(0, 0):
                    c.start()

            for c in fetches(i, slot):
                c.wait()

            @pl.when(i + 1 < n_tiles)
            def _():
                for c in fetches(i + 1, 1 - slot):
                    c.start()

        if batch_major_out:
            @pl.when(i >= 2)
            def _():
                for c in writebacks(i - 2, slot):
                    c.wait()

        pst = min(NORM_STEPS, steps)

        def piece(s, carry):
            t0 = pl.multiple_of(s * pst, pst)
            r0 = pl.multiple_of(s * pst * BATCH, pst * BATCH)
            if batch_major_in:
                x = xbuf[slot, pl.ds(t0, pst), :, :].reshape(pst * BATCH, D_MODEL)
            else:
                x = x_ref[pl.ds(r0, pst * BATCH), :]
            if batch_major_out:
                acc[slot, pl.ds(t0, pst), :, :] = x.reshape(pst, BATCH, D_MODEL)
            else:
                o_ref[pl.ds(r0, pst * BATCH), :] = x
            n_ref[pl.ds(r0, pst * BATCH), :] = _rms_norm(x, gain_ref[...]).astype(BF16)
            return carry

        lax.fori_loop(0, steps // pst, piece, 0)

    wg, wu, wo = wg_ref[...], wu_ref[...], wo_ref[...]
    if f32_weights:
        wg, wu, wo = wg.astype(BF16), wu.astype(BF16), wo.astype(BF16)
        for dst, w in zip(bf16_copies, (wg, wu, wo)):
            dst[0] = w

    n = n_ref[...]
    g = jnp.dot(n, wg, preferred_element_type=F32)
    u = jnp.dot(n, wu, preferred_element_type=F32)
    act = (g * (0.25 * jnp.tanh(0.5 * g) + 0.25)) * u
    update = jnp.dot(act.astype(BF16), wo, preferred_element_type=F32)
    if batch_major_out:
        acc[slot] += update.reshape(steps, BATCH, D_MODEL)
    else:
        o_ref[...] += update

    if batch_major_out:
        @pl.when(j == last_j)
        def _():
            if final_norm:
                acc[slot] = _rms_norm(acc[slot], fgain_ref[...])
            for c in writebacks(i, slot):
                c.start()

            @pl.when(i == n_tiles - 1)
            def _():
                if n_tiles >= 2:
                    for c in writebacks(i - 1, 1 - slot):
                        c.wait()
                for c in writebacks(i, slot):
                    c.wait()
    elif final_norm:
        @pl.when(j == last_j)
        def _():
            o_ref[...] = _rms_norm(o_ref[...], fgain_ref[...])


def _cast_slab_specs(w, first_layer, n_tiles, n_chunks):
    layers, r, c = w.shape
    n = layers - first_layer
    assert first_layer % n == 0
    if r % (n_tiles * BF16_TILE_ROWS) == 0 and c % (n_chunks * LANES) == 0:
        block = (n, r // n_tiles, c // n_chunks)
        pos = lambda i, j: (i, j)
    else:
        assert r % (n_tiles * n_chunks * BF16_TILE_ROWS) == 0 and c % LANES == 0
        block = (n, r // (n_tiles * n_chunks), c)
        pos = lambda i, j: (i * n_chunks + j, 0)
    return (pl.BlockSpec(block, lambda i, j: (first_layer // n,) + pos(i, j)),
            pl.BlockSpec(block, lambda i, j: (0,) + pos(i, j)),
            jax.ShapeDtypeStruct((n, r, c), BF16))


def _ffn(h, gain, weights, layer, *, rows, batch_major_in=False, batch_major_out=False,
         final_gain=None, cast=(), f32_weights=False):
    w_gate, w_up, up_chunk, w_out = weights
    m = h.shape[1] * BATCH if batch_major_in else h.shape[0]
    steps = rows // BATCH
    n_tiles = m // rows
    nf = D_FF // FFN_COLS
    final_norm = final_gain is not None
    row_spec = pl.BlockSpec((rows, D_MODEL), lambda i, j: (i, 0))
    hbm_spec = pl.BlockSpec(memory_space=pl.ANY)
    in_specs = [
        hbm_spec if batch_major_in else row_spec,
        pl.BlockSpec((1, D_MODEL), lambda i, j: (0, 0)),
        pl.BlockSpec((None, D_MODEL, FFN_COLS), lambda i, j: (layer, 0, j)),
        pl.BlockSpec((None, D_MODEL, FFN_COLS), lambda i, j: (layer, 0, up_chunk + j)),
        pl.BlockSpec((None, FFN_COLS, D_MODEL), lambda i, j: (layer, j, 0)),
    ]
    args = [h, gain, w_gate, w_up, w_out]
    if final_norm:
        in_specs.append(pl.BlockSpec((1, D_MODEL), lambda i, j: (0, 0)))
        args.append(final_gain)
    cast_specs = [_cast_slab_specs(w, first, n_tiles, nf) for w, first in cast]
    in_specs += [c[0] for c in cast_specs]
    args += [w for w, _ in cast]
    out_shape = (BATCH, m // BATCH, D_MODEL) if batch_major_out else (m, D_MODEL)
    scratch = [pltpu.VMEM((rows, D_MODEL), BF16)]
    tile_buffers = [pltpu.VMEM((2, steps, BATCH, D_MODEL), F32), pltpu.SemaphoreType.DMA((2,))]
    if batch_major_in:
        scratch += tile_buffers
    if batch_major_out:
        scratch += tile_buffers
    manual = batch_major_in or batch_major_out
    copy_specs, copy_shapes = [], []
    if f32_weights:
        assert n_tiles == 1
        copy_specs = [pl.BlockSpec((1, D_MODEL, FFN_COLS), lambda i, j: (0, 0, j)),
                      pl.BlockSpec((1, D_MODEL, FFN_COLS), lambda i, j: (0, 0, j)),
                      pl.BlockSpec((1, FFN_COLS, D_MODEL), lambda i, j: (0, j, 0))]
        copy_shapes = [jax.ShapeDtypeStruct((1, D_MODEL, D_FF), BF16),
                       jax.ShapeDtypeStruct((1, D_MODEL, D_FF), BF16),
                       jax.ShapeDtypeStruct((1, D_FF, D_MODEL), BF16)]
    outs = pl.pallas_call(
        functools.partial(_ffn_kernel, rows=rows, n_tiles=n_tiles, batch_major_in=batch_major_in,
                          batch_major_out=batch_major_out, final_norm=final_norm,
                          n_cast=len(cast), f32_weights=f32_weights),
        grid=(n_tiles, nf),
        in_specs=in_specs,
        out_specs=([hbm_spec if batch_major_out else row_spec] + [c[1] for c in cast_specs]
                   + copy_specs),
        out_shape=([jax.ShapeDtypeStruct(out_shape, F32)] + [c[2] for c in cast_specs]
                   + copy_shapes),
        scratch_shapes=scratch,
        compiler_params=pltpu.CompilerParams(
            dimension_semantics=("arbitrary" if manual else "parallel", "arbitrary"),
            vmem_limit_bytes=V7X_VMEM_LIMIT_BYTES),
        name="ffn" + ("_from_bt" if batch_major_in else "") + ("_to_bt" if batch_major_out else ""),
    )(*args)
    return tuple(outs) if len(outs) > 1 else outs[0]


def _ffn_wide_kernel(x_ref, gain_ref, wgate_ref, wup_ref, wout_ref, *rest, rows, n_tiles, n_chunks,
                     layer, up_chunk, batch_major_out, final_norm):
    rest = list(rest)
    fgain_ref = rest.pop(0) if final_norm else None
    o_ref, n_ref, xbuf, x_sem, wg_buf, wu_buf, wo_buf, w_sem = (rest.pop(0) for _ in range(8))
    acc, out_sem = (rest.pop(0), rest.pop(0)) if batch_major_out else (None, None)
    assert not rest
    i = pl.program_id(0)
    j = pl.program_id(1)
    steps = rows // BATCH
    n_steps = (n_chunks + 1) // 2
    last_is_narrow = n_chunks % 2 == 1
    wide_cols = 2 * FFN_COLS
    slot = lax.rem(i * n_steps + j, 2)
    acc_slot = lax.rem(i, 2)

    def x_fetch(tile):
        return pltpu.make_async_copy(x_ref.at[pl.ds(tile * steps, steps)], xbuf, x_sem.at[0])

    def weight_fetches(step, into, cols):
        c0 = step * wide_cols
        if not isinstance(c0, int):
            c0 = pl.multiple_of(c0, wide_cols)
        return [
            pltpu.make_async_copy(wgate_ref.at[layer, :, pl.ds(c0, cols)],
                                  wg_buf.at[into, :, pl.ds(0, cols)], w_sem.at[into]),
            pltpu.make_async_copy(wup_ref.at[layer, :, pl.ds(up_chunk * FFN_COLS + c0, cols)],
                                  wu_buf.at[into, :, pl.ds(0, cols)], w_sem.at[into]),
            pltpu.make_async_copy(wout_ref.at[layer, pl.ds(c0, cols), :],
                                  wo_buf.at[into, pl.ds(0, cols), :], w_sem.at[into]),
        ]

    def writebacks(tile, outof):
        return [pltpu.make_async_copy(acc.at[outof, :, b, :],
                                      o_ref.at[b, pl.ds(tile * steps, steps), :], out_sem.at[outof])
                for b in range(BATCH)]

    def cols_of(step):
        return FFN_COLS if (last_is_narrow and step == n_steps - 1) else wide_cols

    @pl.when(j == 0)
    def _():
        @pl.when(i == 0)
        def _():
            x_fetch(0).start()
            for c in weight_fetches(0, 0, cols_of(0)):
                c.start(priority=1)

        x_fetch(i).wait()
        if batch_major_out:
            @pl.when(i >= 2)
            def _():
                for c in writebacks(i - 2, acc_slot):
                    c.wait()

        def piece(s, carry):
            t0 = pl.multiple_of(s * NORM_STEPS, NORM_STEPS)
            r0 = pl.multiple_of(s * NORM_STEPS * BATCH, NORM_STEPS * BATCH)
            x = xbuf[pl.ds(t0, NORM_STEPS), :, :]
            if batch_major_out:
                acc[acc_slot, pl.ds(t0, NORM_STEPS), :, :] = x
            x = x.reshape(NORM_STEPS * BATCH, D_MODEL)
            if not batch_major_out:
                o_ref[pl.ds(r0, NORM_STEPS * BATCH), :] = x
            n_ref[pl.ds(r0, NORM_STEPS * BATCH), :] = _rms_norm(x, gain_ref[...]).astype(BF16)
            return carry

        lax.fori_loop(0, steps // NORM_STEPS, piece, 0)

        @pl.when(i + 1 < n_tiles)
        def _():
            x_fetch(i + 1).start()

    def start_fetches(step, cols):
        for c in weight_fetches(step, 1 - slot, cols):
            c.start(priority=1)

    def step_body(cols):
        nxt = j + 1
        pl.when(nxt < n_steps - 1)(functools.partial(start_fetches, nxt, wide_cols))
        if n_steps > 1:
            pl.when(nxt == n_steps - 1)(functools.partial(start_fetches, nxt, cols_of(n_steps - 1)))
        pl.when((nxt == n_steps) & (i + 1 < n_tiles))(functools.partial(start_fetches, 0, cols_of(0)))
        for c in weight_fetches(j, slot, cols):
            c.wait()

        n = n_ref[...]
        for c0 in range(0, cols, FFN_COLS):
            chunk = slice(c0, c0 + FFN_COLS)
            g = jnp.dot(n, wg_buf[slot, :, chunk], preferred_element_type=F32)
            u = jnp.dot(n, wu_buf[slot, :, chunk], preferred_element_type=F32)
            act = (g * (0.25 * jnp.tanh(0.5 * g) + 0.25)) * u
            update = jnp.dot(act.astype(BF16), wo_buf[slot, chunk, :], preferred_element_type=F32)
            if batch_major_out:
                acc[acc_slot] += update.reshape(steps, BATCH, D_MODEL)
            else:
                o_ref[...] += update

    if last_is_narrow:
        pl.when(j < n_steps - 1)(functools.partial(step_body, wide_cols))
        pl.when(j == n_steps - 1)(functools.partial(step_body, FFN_COLS))
    else:
        step_body(wide_cols)

    if batch_major_out:
        @pl.when(j == n_steps - 1)
        def _():
            if final_norm:
                acc[acc_slot] = _rms_norm(acc[acc_slot], fgain_ref[...])
            for c in writebacks(i, acc_slot):
                c.start()

            @pl.when(i == n_tiles - 1)
            def _():
                if n_tiles >= 2:
                    for c in writebacks(i - 1, 1 - acc_slot):
                        c.wait()
                for c in writebacks(i, acc_slot):
                    c.wait()
    elif final_norm:
        @pl.when(j == n_steps - 1)
        def _():
            o_ref[...] = _rms_norm(o_ref[...], fgain_ref[...])


def _ffn_wide(h, gain, weights, layer, *, rows, batch_major_out=False, final_gain=None):
    w_gate, w_up, up_chunk, w_out = weights
    m = h.shape[0]
    steps = rows // BATCH
    n_tiles = m // rows
    nf = D_FF // FFN_COLS
    final_norm = final_gain is not None
    row_spec = pl.BlockSpec((rows, D_MODEL), lambda i, j: (i, 0))
    hbm_spec = pl.BlockSpec(memory_space=pl.ANY)
    in_specs = [hbm_spec, pl.BlockSpec((1, D_MODEL), lambda i, j: (0, 0)), hbm_spec, hbm_spec, hbm_spec]
    args = [h.reshape(m // BATCH, BATCH, D_MODEL), gain, w_gate, w_up, w_out]
    if final_norm:
        in_specs.append(pl.BlockSpec((1, D_MODEL), lambda i, j: (0, 0)))
        args.append(final_gain)
    out_shape = (BATCH, m // BATCH, D_MODEL) if batch_major_out else (m, D_MODEL)
    scratch = [
        pltpu.VMEM((rows, D_MODEL), BF16),
        pltpu.VMEM((steps, BATCH, D_MODEL), F32), pltpu.SemaphoreType.DMA((1,)),
        pltpu.VMEM((2, D_MODEL, 2 * FFN_COLS), BF16),
        pltpu.VMEM((2, D_MODEL, 2 * FFN_COLS), BF16),
        pltpu.VMEM((2, 2 * FFN_COLS, D_MODEL), BF16),
        pltpu.SemaphoreType.DMA((2,)),
    ]
    if batch_major_out:
        scratch += [pltpu.VMEM((2, steps, BATCH, D_MODEL), F32), pltpu.SemaphoreType.DMA((2,))]
    return pl.pallas_call(
        functools.partial(_ffn_wide_kernel, rows=rows, n_tiles=n_tiles, n_chunks=nf, layer=layer,
                          up_chunk=up_chunk, batch_major_out=batch_major_out, final_norm=final_norm),
        grid=(n_tiles, (nf + 1) // 2),
        in_specs=in_specs,
        out_specs=hbm_spec if batch_major_out else row_spec,
        out_shape=jax.ShapeDtypeStruct(out_shape, F32),
        scratch_shapes=scratch,
        compiler_params=pltpu.CompilerParams(
            dimension_semantics=("arbitrary", "arbitrary"),
            vmem_limit_bytes=V7X_VMEM_LIMIT_BYTES),
        name="ffn_wide" + ("_to_bt" if batch_major_out else ""),
    )(*args)


def _mix_kernel(x_ref, xprev_ref, gain_ref, win_ref, convw_ref, convb_ref, wgate_ref, ba_ref,
                bx_ref, aparam_ref, poolw_ref, poolb_ref, pools_ref, wout_ref,
                h0_ref, ctail0_ref, ptail0_ref,
                o_ref, hstate_ref, ctail_ref, ptail_ref,
                zx_ext, zp_ext, mix_s, *, rows, t_offset):
    s = pl.program_id(0)
    n_chunks = pl.num_programs(0) - 1
    chunk = jnp.minimum(s, n_chunks - 1)
    steps = rows // BATCH

    @pl.when(s == 0)
    def _():
        hstate_ref[...] = h0_ref[...]
        zx_ext[0:CONV_TAIL, :] = ctail0_ref[...]
        zp_ext[0:POOL_TAIL, :] = ptail0_ref[...]
        mix_s[...] = jnp.zeros_like(mix_s)

    n = _rms_norm(x_ref[...], gain_ref[...]).astype(BF16)
    z = jnp.dot(n, win_ref[...], preferred_element_type=F32)
    zx = z[:, :D_LRU]
    zg = z[:, D_LRU:2 * D_LRU]
    zp = z[:, 2 * D_LRU:]

    zx_ext[CONV_TAIL:CONV_TAIL + rows, :] = zx
    xc = convb_ref[...] + convw_ref[CONV_WIDTH - 1:CONV_WIDTH, :] * zx
    for k in range(CONV_WIDTH - 1):
        xc = xc + convw_ref[k:k + 1, :] * zx_ext[k * BATCH:k * BATCH + rows, :]
    ctail = zx_ext[rows:rows + CONV_TAIL, :]
    zx_ext[0:CONV_TAIL, :] = ctail

    ap = aparam_ref[...]
    neg_c_softplus = -LRU_C * (jnp.maximum(-ap, 0.0) + jnp.log1p(jnp.exp(-jnp.abs(ap))))
    xcb = xc.astype(BF16)
    head_cols = [slice(hd * LRU_HEAD_DIM, (hd + 1) * LRU_HEAD_DIM) for hd in range(LRU_HEADS)]
    gates = [jnp.dot(xcb[:, cols], wgate_ref[hd], preferred_element_type=F32)
             for hd, cols in enumerate(head_cols)]

    zp_ext[POOL_TAIL:POOL_TAIL + rows, :] = zp
    pooled = []
    for gidx, win in enumerate(POOL_WINDOWS):
        cols = slice(gidx * POOL_GROUP_DIM, (gidx + 1) * POOL_GROUP_DIM)
        ext = zp_ext[:, cols]
        acc = ext
        span = 1
        while span < win:
            acc = acc[span * BATCH:, :] + acc[:acc.shape[0] - span * BATCH, :]
            span *= 2
        acc = acc[acc.shape[0] - rows:, :]
        u = ext[POOL_TAIL:, :]
        if t_offset + 1 >= win:
            d = acc * (1.0 / win) - u
        else:
            row = lax.broadcasted_iota(jnp.int32, (rows, POOL_GROUP_DIM), 0)
            t_abs = lax.shift_right_logical(row, 3) + (chunk * steps + t_offset)
            d = acc / jnp.minimum(t_abs + 1, win).astype(F32) - u
        pooled.append(jnp.dot(d.astype(BF16), poolw_ref[gidx], preferred_element_type=F32))
    ptail = zp_ext[rows:rows + POOL_TAIL, :]
    zp_ext[0:POOL_TAIL, :] = ptail

    o_ref[...] = xprev_ref[...] + jnp.dot(mix_s[...], wout_ref[...], preferred_element_type=F32)

    h_prev = hstate_ref[...]
    h_new = []
    mixed = []
    for hd, cols in enumerate(head_cols):
        h = h_prev[:, cols]
        pieces = []
        for r0 in range(0, rows, SCAN_STEPS * BATCH):
            blk = slice(r0, r0 + SCAN_STEPS * BATCH)
            r = _sigmoid(gates[hd][blk, :LRU_HEAD_DIM] + ba_ref[:, cols])
            ig = _sigmoid(gates[hd][blk, LRU_HEAD_DIM:] + bx_ref[:, cols])
            t = jnp.tanh(0.5 * (r * neg_c_softplus[:, cols]))
            q = 1.0 / (1.0 - t)
            a = (1.0 + t) * q
            b = (2.0 * q * jnp.sqrt(-t)) * ig * xc[blk, cols]
            ys = []
            for st in range(SCAN_STEPS):
                rs = slice(st * BATCH, (st + 1) * BATCH)
                h = a[rs, :] * h + b[rs, :]
                ys.append(h)
            pieces.append((jnp.concatenate(ys, axis=0) * _gelu_tanh(zg[blk, cols])).astype(BF16))
        h_new.append(h)
        mixed.append(jnp.concatenate(pieces, axis=0))
    for gidx in range(len(POOL_WINDOWS)):
        cols = slice(gidx * POOL_GROUP_DIM, (gidx + 1) * POOL_GROUP_DIM)
        mixed.append(((pooled[gidx] + poolb_ref[:, cols]) * pools_ref[:, cols]).astype(BF16))
    mix_s[...] = jnp.concatenate(mixed, axis=1)

    @pl.when(s < n_chunks)
    def _():
        hstate_ref[...] = jnp.concatenate(h_new, axis=1)
        ctail_ref[...] = ctail
        ptail_ref[...] = ptail


def _mix(h, p, layer, state, *, rows, t_offset):
    m = h.shape[0]
    n_chunks = m // rows
    const = lambda s: (0, 0)

    def resident(shape):
        return pl.BlockSpec((None,) + shape, lambda s: (layer,) + (0,) * len(shape),
                            pipeline_mode=pl.Buffered(1))

    vec = lambda nl: pl.BlockSpec((1, nl), const)
    cur_spec = pl.BlockSpec((rows, D_MODEL), lambda s: (jnp.minimum(s, n_chunks - 1), 0))
    prev_spec = pl.BlockSpec((rows, D_MODEL), lambda s: (jnp.maximum(s - 1, 0), 0))
    in_specs = [
        cur_spec,
        prev_spec,
        vec(D_MODEL),
        resident((D_MODEL, D_IN)),
        pl.BlockSpec((CONV_WIDTH, D_LRU), const),
        vec(D_LRU),
        resident((LRU_HEADS, LRU_HEAD_DIM, 2 * LRU_HEAD_DIM)),
        vec(D_LRU), vec(D_LRU), vec(D_LRU),
        resident((len(POOL_WINDOWS), POOL_GROUP_DIM, POOL_GROUP_DIM)),
        vec(D_POOL), vec(D_POOL),
        resident((D_MIX, D_MODEL)),
        pl.BlockSpec((BATCH, D_LRU), const),
        pl.BlockSpec((CONV_TAIL, D_LRU), const),
        pl.BlockSpec((POOL_TAIL, D_POOL), const),
    ]
    out_specs = [
        prev_spec,
        pl.BlockSpec((BATCH, D_LRU), const),
        pl.BlockSpec((CONV_TAIL, D_LRU), const),
        pl.BlockSpec((POOL_TAIL, D_POOL), const),
    ]
    out_shape = [
        jax.ShapeDtypeStruct((m, D_MODEL), F32),
        jax.ShapeDtypeStruct((BATCH, D_LRU), F32),
        jax.ShapeDtypeStruct((CONV_TAIL, D_LRU), F32),
        jax.ShapeDtypeStruct((POOL_TAIL, D_POOL), F32),
    ]
    scratch = [
        pltpu.VMEM((rows + CONV_TAIL, D_LRU), F32),
        pltpu.VMEM((rows + POOL_TAIL, D_POOL), F32),
        pltpu.VMEM((rows, D_MIX), BF16),
    ]
    out, hstate, ctail, ptail = pl.pallas_call(
        functools.partial(_mix_kernel, rows=rows, t_offset=t_offset),
        grid=(n_chunks + 1,),
        in_specs=in_specs,
        out_specs=out_specs,
        out_shape=out_shape,
        scratch_shapes=scratch,
        compiler_params=pltpu.CompilerParams(
            dimension_semantics=("arbitrary",),
            vmem_limit_bytes=V7X_VMEM_LIMIT_BYTES),
        name="mix",
    )(h, h, p["gain"], p["w_in"], p["conv_w"], p["conv_b"], p["w_gate"], p["ba"], p["bx"],
      p["a_param"], p["pool_w"], p["pool_b"], p["pool_scale"], p["w_out"], *state)
    return out, (hstate, ctail, ptail)


def kernel(x, meta_tokens, ffn1_norm, ffn1_w_in, ffn1_w_out, mix_norm, w_in, conv_w, conv_b,
           lru_wa, lru_ba, lru_wx, lru_bx, lru_a_param, pool_w, pool_b, pool_scale, w_out,
           ffn2_norm, ffn2_w_in, ffn2_w_out, final_norm):
    b, t, d = x.shape
    assert (b, d) == (BATCH, D_MODEL) and (t * b) % FFN_ROWS == 0 and (t * b) % MIX_ROWS == 0

    hm = jnp.broadcast_to(meta_tokens.astype(x.dtype)[:, None, :], (N_META, b, d)).reshape(N_META * b, d)
    meta_rows = N_META * b
    hx = x

    row = lambda v: v.reshape(1, -1)
    zero_state = (jnp.zeros((BATCH, D_LRU), F32), jnp.zeros((CONV_TAIL, D_LRU), F32),
                  jnp.zeros((POOL_TAIL, D_POOL), F32))
    nf = D_FF // FFN_COLS
    mix_w = dict(w_in=w_in.astype(BF16),
                 w_gate=jnp.concatenate([lru_wa, lru_wx], axis=-1).astype(BF16),
                 pool_w=pool_w.astype(BF16), w_out=w_out.astype(BF16))
    hm, f1_gate, f1_up, f1_out = _ffn(hm, row(ffn1_norm[0]), (ffn1_w_in, ffn1_w_in, nf, ffn1_w_out),
                                      0, rows=meta_rows, f32_weights=True)
    f1_first = (f1_gate, f1_up, 0, f1_out)
    hx, f1_in_rest, f1_out_rest, f2_in, f2_out = _ffn(
        hx, row(ffn1_norm[0]), f1_first, 0, rows=FFN_ROWS, batch_major_in=True,
        cast=((ffn1_w_in, 1), (ffn1_w_out, 1), (ffn2_w_in, 0), (ffn2_w_out, 0)))
    f1_rest = (f1_in_rest, f1_in_rest, nf, f1_out_rest)
    f2_w = (f2_in, f2_in, nf, f2_out)

    for l in range(DEPTH):
        last = l == DEPTH - 1
        mp = dict(mix_w, gain=row(mix_norm[l]), conv_w=conv_w[l], conv_b=row(conv_b[l]),
                  ba=row(lru_ba[l]), bx=row(lru_bx[l]), a_param=row(lru_a_param[l]),
                  pool_b=row(pool_b[l]), pool_scale=row(pool_scale[l]))

        if l > 0:
            hm = _ffn(hm, row(ffn1_norm[l]), f1_rest, l - 1, rows=meta_rows)
            hx = _ffn_wide(hx, row(ffn1_norm[l]), f1_rest, l - 1, rows=FFN_ROWS)
        hm, meta_state = _mix(hm, mp, l, zero_state, rows=meta_rows, t_offset=0)
        hx, _ = _mix(hx, mp, l, meta_state, rows=MIX_ROWS, t_offset=N_META)
        if not last:
            hm = _ffn(hm, row(ffn2_norm[l]), f2_w, l, rows=meta_rows)
        hx = _ffn_wide(hx, row(ffn2_norm[l]), f2_w, l, rows=FFN_ROWS, batch_major_out=last,
                       final_gain=row(final_norm) if last else None)

    return hx
```
